```python
import jax
import jax.numpy as jnp
from jax import lax
import numpy as np

D_MODEL = 1024
BATCH = 16
SEQ = 2048
DEPTH = 2

CHUNK = 64
Q_BLOCK = 128
RMS_EPS = 1e-6
D_FF = 2816

FOX_HEADS = 8
FOX_HEAD_DIM = 64

MLA_HEADS = 8
MLA_Q_RANK = 256
MLA_KV_RANK = 128
MLA_NOPE_DIM = 64
MLA_ROPE_DIM = 32
MLA_V_DIM = 64
ROPE_THETA = 10000.0

GDN_HEADS = 8
GDN_HEAD_DIM = 128
GDN_CONV = 4

N_EVEN = (DEPTH + 1) // 2
N_ODD = DEPTH // 2

FOX_WIDTH = FOX_HEADS * FOX_HEAD_DIM
ATTN_SPLITS = (FOX_WIDTH, FOX_WIDTH, FOX_WIDTH, FOX_HEADS, MLA_Q_RANK, MLA_KV_RANK, MLA_ROPE_DIM)
ATTN_IN = sum(ATTN_SPLITS)
ATTN_OUT = FOX_WIDTH + MLA_HEADS * MLA_V_DIM
GDN_WIDTH = GDN_HEADS * GDN_HEAD_DIM
GDN_SPLITS = (3 * GDN_WIDTH, GDN_HEADS, GDN_HEADS, GDN_WIDTH)
GDN_IN = sum(GDN_SPLITS)

kernel_name = "hybrid_fox_mla_gdn_macaron"


def _split(t, sizes):
    idx = np.cumsum(sizes)[:-1].tolist()
    return jnp.split(t, idx, axis=-1)


def _rms(x, g):
    xf = x.astype(jnp.float32)
    y = xf * lax.rsqrt(jnp.mean(xf * xf, axis=-1, keepdims=True) + RMS_EPS)
    return (y * g.astype(jnp.float32)).astype(x.dtype)


def _l2norm(x):
    return x * lax.rsqrt(jnp.sum(x * x, axis=-1, keepdims=True) + 1e-6)


def _swiglu(h, w13, w2):
    a, b = jnp.split(h @ w13, 2, axis=-1)
    return (jax.nn.silu(a) * b) @ w2


def _rope(x, pos):
    half = x.shape[-1] // 2
    inv = ROPE_THETA ** (-jnp.arange(half, dtype=jnp.float32) / half)
    ang = pos.astype(jnp.float32)[:, None] * inv[None, :]
    cos, sin = jnp.cos(ang), jnp.sin(ang)
    xf = x.astype(jnp.float32)
    x1, x2 = xf[..., :half], xf[..., half:]
    return jnp.concatenate([x1 * cos - x2 * sin, x2 * cos + x1 * sin], axis=-1).astype(x.dtype)


def _blocked_attention(q, k, v, scale, chunk_causal, log_forget_cum=None):
    s_len = q.shape[2]
    neg = jnp.finfo(jnp.float32).min
    outs = []
    for start in range(0, s_len, Q_BLOCK):
        end = start + Q_BLOCK
        scores = jnp.einsum("bhqd,bhkd->bhqk", q[:, :, start:end], k[:, :, :end],
                            preferred_element_type=jnp.float32) * scale
        t_pos = jnp.arange(start, end)[:, None]
        s_pos = jnp.arange(end)[None, :]
        if chunk_causal:
            mask = (s_pos // CHUNK) <= (t_pos // CHUNK)
        else:
            mask = s_pos <= t_pos
        if log_forget_cum is not None:
            scores = scores + log_forget_cum[:, :, start:end, None] - log_forget_cum[:, :, None, :end]
        probs = jax.nn.softmax(jnp.where(mask, scores, neg), axis=-1)
        outs.append(jnp.einsum("bhqk,bhkd->bhqd", probs.astype(v.dtype), v[:, :, :end]))
    return jnp.concatenate(outs, axis=2)


def _fox_mla_mixer(h, w_in, f_bias, q_norm, w_uq, kv_norm, w_ukv, w_out):
    b, s, _ = h.shape
    pos = jnp.arange(s)
    q_a, k_a, v_a, f_logit, c_q, c_kv, k_pe = _split(h @ w_in, ATTN_SPLITS)

    def heads(t, n):
        return t.reshape(b, s, n, -1).transpose(0, 2, 1, 3)

    log_f = jax.nn.log_sigmoid(f_logit.astype(jnp.float32) + f_bias.astype(jnp.float32))
    f_cum = jnp.cumsum(log_f, axis=1).transpose(0, 2, 1)
    o_a = _blocked_attention(heads(q_a, FOX_HEADS), heads(k_a, FOX_HEADS), heads(v_a, FOX_HEADS),
                             FOX_HEAD_DIM ** -0.5, False, f_cum)

    q_b = heads(_rms(c_q, q_norm) @ w_uq, MLA_HEADS)
    kv_b = heads(_rms(c_kv, kv_norm) @ w_ukv, MLA_HEADS)
    q_nope, q_pe = q_b[..., :MLA_NOPE_DIM], q_b[..., MLA_NOPE_DIM:]
    k_nope, v_b = kv_b[..., :MLA_NOPE_DIM], kv_b[..., MLA_NOPE_DIM:]
    k_pe = _rope(k_pe[:, None], pos)
    q_full = jnp.concatenate([q_nope, _rope(q_pe, pos)], axis=-1)
    k_full = jnp.concatenate([k_nope, jnp.broadcast_to(k_pe, (b, MLA_HEADS, s, MLA_ROPE_DIM))], axis=-1)
    o_b = _blocked_attention(q_full, k_full, v_b, (MLA_NOPE_DIM + MLA_ROPE_DIM) ** -0.5, True)

    def merge(o):
        return o.transpose(0, 2, 1, 3).reshape(b, s, -1)

    return jnp.concatenate([merge(o_a), merge(o_b)], axis=-1) @ w_out


def _causal_conv(x, w):
    k_len, c = w.shape
    return lax.conv_general_dilated(x, w[:, None, :].astype(x.dtype), window_strides=(1,),
                                    padding=[(k_len - 1, 0)],
                                    dimension_numbers=("NWC", "WIO", "NWC"),
                                    feature_group_count=c)


def _chunk_gated_delta(q, k, v, g, beta):
    b, h, s, dk = q.shape
    dv = v.shape[-1]
    n = s // CHUNK
    q = q.reshape(b, h, n, CHUNK, dk)
    k = k.reshape(b, h, n, CHUNK, dk)
    v = v.reshape(b, h, n, CHUNK, dv)
    g = jnp.cumsum(g.reshape(b, h, n, CHUNK), axis=-1)
    beta = beta.reshape(b, h, n, CHUNK)
    idx = jnp.arange(CHUNK)
    incl = idx[:, None] >= idx[None, :]
    strict = idx[:, None] > idx[None, :]
    decay = jnp.exp(jnp.where(incl, g[..., :, None] - g[..., None, :], -jnp.inf))
    k_beta = k * beta[..., None]
    m = jnp.where(strict, jnp.einsum("bhncd,bhnjd->bhncj", k_beta, k) * decay, 0.0)
    eye = jnp.eye(CHUNK, dtype=jnp.float32)
    t_mat = lax.linalg.triangular_solve(eye + m, jnp.broadcast_to(eye, m.shape),
                                        left_side=True, lower=True, unit_diagonal=True)
    u = jnp.einsum("bhncj,bhnjd->bhncd", t_mat, v * beta[..., None])
    w_dec = jnp.einsum("bhncj,bhnjd->bhncd", t_mat, k_beta * jnp.exp(g)[..., None])
    intra = jnp.einsum("bhncd,bhnjd->bhncj", q, k) * decay
    q_dec = q * jnp.exp(g)[..., None]
    k_tail = k * jnp.exp(g[..., -1:] - g)[..., None]
    g_last = jnp.exp(g[..., -1])

    def step(state, xs):
        u_c, w_c, intra_c, q_c, kt_c, gl_c = xs
        v_new = u_c - jnp.einsum("bhcd,bhde->bhce", w_c, state)
        out = jnp.einsum("bhcd,bhde->bhce", q_c, state) + jnp.einsum("bhcj,bhje->bhce", intra_c, v_new)
        state = state * gl_c[..., None, None] + jnp.einsum("bhcd,bhce->bhde", kt_c, v_new)
        return state, out

    xs = (jnp.moveaxis(u, 2, 0), jnp.moveaxis(w_dec, 2, 0), jnp.moveaxis(intra, 2, 0),
          jnp.moveaxis(q_dec, 2, 0), jnp.moveaxis(k_tail, 2, 0), jnp.moveaxis(g_last, 2, 0))
    state0 = jnp.zeros((b, h, dk, dv), jnp.float32)
    _, out = lax.scan(step, state0, xs)
    return jnp.moveaxis(out, 0, 2).reshape(b, h, s, dv)


def _gdn_mixer(h, w_in, conv_w, a_log, dt_bias, out_norm, w_out):
    b, s, _ = h.shape
    qkv, b_logit, a_logit, gate = _split(h @ w_in, GDN_SPLITS)
    qkv = jax.nn.silu(_causal_conv(qkv, conv_w)).astype(jnp.float32)
    q, k, v = jnp.split(qkv, 3, axis=-1)

    def heads(t):
        return t.reshape(b, s, GDN_HEADS, GDN_HEAD_DIM).transpose(0, 2, 1, 3)

    q = _l2norm(heads(q)) * GDN_HEAD_DIM ** -0.5
    k = _l2norm(heads(k))
    v = heads(v)
    beta = jax.nn.sigmoid(b_logit.astype(jnp.float32)).transpose(0, 2, 1)
    g = -jnp.exp(a_log.astype(jnp.float32)) * jax.nn.softplus(
        a_logit.astype(jnp.float32) + dt_bias.astype(jnp.float32))
    g = g.transpose(0, 2, 1)
    o = _chunk_gated_delta(q, k, v, g, beta).transpose(0, 2, 1, 3)
    gate = gate.astype(jnp.float32).reshape(b, s, GDN_HEADS, GDN_HEAD_DIM)
    o = _rms(o, out_norm) * jax.nn.silu(gate)
    return o.reshape(b, s, GDN_WIDTH).astype(h.dtype) @ w_out


def setup_inputs(seed: int = 0) -> dict:
    key = jax.random.key(seed)
    ks = jax.random.split(key, 24)
    f32 = jnp.float32

    def w(k, shape, fan_in):
        return jax.random.normal(k, shape, f32) * fan_in ** -0.5

    def gain(k, shape):
        return 1.0 + 0.02 * jax.random.normal(k, shape, f32)

    dt = jnp.exp(float(np.log(1e-3)) + jax.random.uniform(ks[15], (N_ODD, GDN_HEADS), f32)
                 * float(np.log(1e-1) - np.log(1e-3)))
    return {
        "x": jax.random.normal(ks[0], (BATCH, SEQ, D_MODEL), f32),
        "ffn1_norm": gain(ks[1], (DEPTH, D_MODEL)),
        "ffn1_w13": w(ks[2], (DEPTH, D_MODEL, 2 * D_FF), D_MODEL),
        "ffn1_w2": w(ks[3], (DEPTH, D_FF, D_MODEL), D_FF),
        "mix_norm": gain(ks[4], (DEPTH, D_MODEL)),
        "attn_w_in": w(ks[5], (N_EVEN, D_MODEL, ATTN_IN), D_MODEL),
        "fox_f_bias": 2.0 + 0.1 * jax.random.normal(ks[6], (N_EVEN, FOX_HEADS), f32),
        "mla_q_norm": gain(ks[7], (N_EVEN, MLA_Q_RANK)),
        "mla_w_uq": w(ks[8], (N_EVEN, MLA_Q_RANK, MLA_HEADS * (MLA_NOPE_DIM + MLA_ROPE_DIM)), MLA_Q_RANK),
        "mla_kv_norm": gain(ks[9], (N_EVEN, MLA_KV_RANK)),
        "mla_w_ukv": w(ks[10], (N_EVEN, MLA_KV_RANK, MLA_HEADS * (MLA_NOPE_DIM + MLA_V_DIM)), MLA_KV_RANK),
        "attn_w_out": w(ks[11], (N_EVEN, ATTN_OUT, D_MODEL), ATTN_OUT),
        "gdn_w_in": w(ks[12], (N_ODD, D_MODEL, GDN_IN), D_MODEL),
        "gdn_conv_w": w(ks[13], (N_ODD, GDN_CONV, 3 * GDN_WIDTH), GDN_CONV),
        "gdn_a_log": jnp.log(jax.random.uniform(ks[14], (N_ODD, GDN_HEADS), f32, minval=1.0, maxval=16.0)),
        "gdn_dt_bias": dt + jnp.log(-jnp.expm1(-dt)),
        "gdn_out_norm": gain(ks[16], (N_ODD, GDN_HEAD_DIM)),
        "gdn_w_out": w(ks[17], (N_ODD, GDN_WIDTH, D_MODEL), GDN_WIDTH),
        "ffn2_norm": gain(ks[18], (DEPTH, D_MODEL)),
        "ffn2_w13": w(ks[19], (DEPTH, D_MODEL, 2 * D_FF), D_MODEL),
        "ffn2_w2": w(ks[20], (DEPTH, D_FF, D_MODEL), D_FF),
        "final_norm": gain(ks[21], (D_MODEL,)),
    }


def reference(x, ffn1_norm, ffn1_w13, ffn1_w2, mix_norm, attn_w_in, fox_f_bias, mla_q_norm,
              mla_w_uq, mla_kv_norm, mla_w_ukv, attn_w_out, gdn_w_in, gdn_conv_w, gdn_a_log,
              gdn_dt_bias, gdn_out_norm, gdn_w_out, ffn2_norm, ffn2_w13, ffn2_w2, final_norm):
    for layer in range(DEPTH):
        x = x + 0.5 * _swiglu(_rms(x, ffn1_norm[layer]), ffn1_w13[layer], ffn1_w2[layer])
        h = _rms(x, mix_norm[layer])
        i = layer // 2
        if layer % 2 == 0:
            x = x + _fox_mla_mixer(h, attn_w_in[i], fox_f_bias[i], mla_q_norm[i], mla_w_uq[i],
                                   mla_kv_norm[i], mla_w_ukv[i], attn_w_out[i])
        else:
            x = x + _gdn_mixer(h, gdn_w_in[i], gdn_conv_w[i], gdn_a_log[i], gdn_dt_bias[i],
                               gdn_out_norm[i], gdn_w_out[i])
        x = x + 0.5 * _swiglu(_rms(x, ffn2_norm[layer]), ffn2_w13[layer], ffn2_w2[layer])
    return _rms(x, final_norm)
```

```python
import functools

import numpy as np
import jax
import jax.numpy as jnp
from jax import lax
from jax.experimental import pallas as pl
from jax.experimental.pallas import tpu as pltpu

F32 = jnp.float32
BF16 = jnp.bfloat16

CHUNK = 64
RMS_EPS = 1e-6
FOX_HEADS = 8
FOX_HEAD_DIM = 64
MLA_HEADS = 8
MLA_Q_RANK = 256
MLA_KV_RANK = 128
MLA_NOPE_DIM = 64
MLA_ROPE_DIM = 32
MLA_V_DIM = 64
ROPE_THETA = 10000.0
GDN_HEADS = 8
GDN_HEAD_DIM = 128
GDN_CONV = 4

LANES = 128
VMEM_LIMIT = 56 * 1024 * 1024
ROW_TILE = 512
ATTN_Q_TILE = 256
NEG = float(jnp.finfo(jnp.float32).min)


def _params(sem):
    return pltpu.CompilerParams(dimension_semantics=sem, vmem_limit_bytes=VMEM_LIMIT)


def _rms_rows(x, g):
    return x * lax.rsqrt(jnp.mean(x * x, axis=-1, keepdims=True) + RMS_EPS) * g


def _dot(a, b):
    return jnp.dot(a, b, preferred_element_type=F32)


def _dot_nt(a, b):
    return lax.dot_general(a, b, (((1,), (1,)), ((), ())), preferred_element_type=F32)


def _const_spec(shape):
    return pl.BlockSpec(shape, lambda *_: (0,) * len(shape), pipeline_mode=pl.Buffered(1))


def _row_scan(y, n_steps, period=None):
    rows = lax.broadcasted_iota(jnp.int32, y.shape, 0)
    if period is not None:
        rows = rows % period
    shift = 1
    for _ in range(n_steps):
        y = y + jnp.where(rows >= shift, pltpu.roll(y, shift, axis=0), 0.0)
        shift *= 2
    return y


def _ffn_body(*refs, d_ff, n_chunks, final):
    if final:
        x_ref, g_ref, w13_ref, w2_ref, fg_ref, o_ref, t_ref = refs
    else:
        x_ref, g_ref, w13_ref, w2_ref, o_ref, t_ref = refs
    x = x_ref[...]
    h = _rms_rows(x, g_ref[...]).astype(BF16)
    fc = d_ff // n_chunks
    for j in range(n_chunks):
        a = _dot(h, w13_ref[:, j * fc:(j + 1) * fc])
        b = _dot(h, w13_ref[:, d_ff + j * fc:d_ff + (j + 1) * fc])
        t_ref[:, j * fc:(j + 1) * fc] = (a * jax.nn.sigmoid(a) * b).astype(BF16)
    out = x + 0.5 * _dot(t_ref[...], w2_ref[...])
    if final:
        out = _rms_rows(out, fg_ref[...])
    o_ref[...] = out


def _ffn(x2, g, w13, w2, final_g=None):
    n, d = x2.shape
    d_ff = w2.shape[0]
    tm = min(ROW_TILE, n)
    final = final_g is not None
    args = [x2, g.reshape(1, d), w13.astype(BF16), w2.astype(BF16)]
    specs = [pl.BlockSpec((tm, d), lambda i: (i, 0)), _const_spec((1, d)),
             _const_spec((d, 2 * d_ff)), _const_spec((d_ff, d))]
    if final:
        args.append(final_g.reshape(1, d))
        specs.append(_const_spec((1, d)))
    return pl.pallas_call(
        functools.partial(_ffn_body, d_ff=d_ff, n_chunks=d_ff // 256, final=final),
        out_shape=jax.ShapeDtypeStruct((n, d), F32),
        grid=(n // tm,),
        in_specs=specs,
        out_specs=pl.BlockSpec((tm, d), lambda i: (i, 0)),
        scratch_shapes=[pltpu.VMEM((tm, d_ff), BF16)],
        compiler_params=_params(("parallel",)),
        name="ffn",
    )(*args)


def _attn_in_body(x_ref, g_ref, w_ref, fb_ref, qn_ref, wq_ref, kvn_ref, wkv_ref,
                  ck_ref, sk_ref, cq_ref, sq_ref,
                  qf_ref, kf_ref, vf_ref, fc_ref, qm_ref, km_ref, vm_ref, carry_ref,
                  *, tiles_per_seq, scan_steps):
    wf = FOX_HEADS * FOX_HEAD_DIM
    h = _rms_rows(x_ref[...], g_ref[...]).astype(BF16)
    y = _dot(h, w_ref[...])
    qf_ref[...] = (y[:, :wf] * FOX_HEAD_DIM ** -0.5).astype(BF16)
    kf_ref[...] = y[:, wf:2 * wf].astype(BF16)
    vf_ref[...] = y[:, 2 * wf:3 * wf].astype(BF16)
    o = 3 * wf
    c_q = y[:, o:o + MLA_Q_RANK]
    o += MLA_Q_RANK
    c_kv = y[:, o:o + MLA_KV_RANK]
    o += MLA_KV_RANK
    kp = y[:, o:o + LANES]
    kq = y[:, o + LANES:o + 2 * LANES]
    f_logit = y[:, o + 2 * LANES:o + 3 * LANES]

    z = f_logit + fb_ref[...]
    log_f = jnp.minimum(z, 0.0) - jnp.log1p(jnp.exp(-jnp.abs(z)))

    @pl.when(pl.program_id(0) % tiles_per_seq == 0)
    def _():
        carry_ref[...] = jnp.zeros_like(carry_ref)

    f_cum = _row_scan(log_f, scan_steps) + carry_ref[...]
    fc_ref[...] = f_cum
    carry_ref[...] = f_cum[-1:, :]

    nh = MLA_HEADS
    qb = _dot(_rms_rows(c_q, qn_ref[...]).astype(BF16), wq_ref[...])
    cq = jnp.concatenate([cq_ref[...]] * nh, axis=1)
    sq = jnp.concatenate([sq_ref[...]] * nh, axis=1)
    qm_ref[...] = (qb[:, :nh * LANES] * cq + qb[:, nh * LANES:] * sq).astype(BF16)
    kvb = _dot(_rms_rows(c_kv, kvn_ref[...]).astype(BF16), wkv_ref[...])
    k_rot = kp * ck_ref[...] + kq * sk_ref[...]
    km_ref[...] = (kvb[:, :nh * LANES] + jnp.concatenate([k_rot] * nh, axis=1)).astype(BF16)
    vm_ref[...] = kvb[:, nh * LANES:].astype(BF16)


def _attn_in(x2, seq, g, w_in, f_bias, q_norm, w_uq, kv_norm, w_ukv):
    n, d = x2.shape
    tm = min(ROW_TILE, seq)
    wf = FOX_HEADS * FOX_HEAD_DIM
    half = MLA_ROPE_DIM // 2
    nh = MLA_HEADS

    qa, ka, va, fl, cq, ckv, kpe = jnp.split(
        w_in, np.cumsum([wf, wf, wf, FOX_HEADS, MLA_Q_RANK, MLA_KV_RANK]).tolist(), axis=1)
    z = lambda c: jnp.zeros((d, c), F32)
    p_blk = jnp.concatenate([z(MLA_NOPE_DIM), kpe, z(LANES - MLA_NOPE_DIM - MLA_ROPE_DIM)], axis=1)
    q_blk = jnp.concatenate([z(MLA_NOPE_DIM), kpe[:, half:], kpe[:, :half],
                             z(LANES - MLA_NOPE_DIM - MLA_ROPE_DIM)], axis=1)
    f_blk = jnp.concatenate([fl, z(LANES - FOX_HEADS)], axis=1)
    w1 = jnp.concatenate([qa, ka, va, cq, ckv, p_blk, q_blk, f_blk], axis=1).astype(BF16)
    n1 = w1.shape[1]
    fb = jnp.concatenate([f_bias, jnp.zeros((LANES - FOX_HEADS,), F32)]).reshape(1, LANES)

    dq = MLA_NOPE_DIM + MLA_ROPE_DIM
    wq3 = w_uq.reshape(MLA_Q_RANK, nh, dq)
    zq = jnp.zeros((MLA_Q_RANK, nh, LANES - dq), F32)
    wa = jnp.concatenate([wq3, zq], axis=2).reshape(MLA_Q_RANK, nh * LANES)
    wb = jnp.concatenate([jnp.zeros((MLA_Q_RANK, nh, MLA_NOPE_DIM), F32),
                          wq3[:, :, MLA_NOPE_DIM + half:], wq3[:, :, MLA_NOPE_DIM:MLA_NOPE_DIM + half],
                          zq], axis=2).reshape(MLA_Q_RANK, nh * LANES)
    wq = jnp.concatenate([wa, wb], axis=1).astype(BF16)
    wkv3 = w_ukv.reshape(MLA_KV_RANK, nh, MLA_NOPE_DIM + MLA_V_DIM)
    wk = jnp.concatenate([wkv3[:, :, :MLA_NOPE_DIM],
                          jnp.zeros((MLA_KV_RANK, nh, LANES - MLA_NOPE_DIM), F32)], axis=2)
    wkv = jnp.concatenate([wk.reshape(MLA_KV_RANK, nh * LANES),
                           wkv3[:, :, MLA_NOPE_DIM:].reshape(MLA_KV_RANK, nh * MLA_V_DIM)],
                          axis=1).astype(BF16)

    inv = ROPE_THETA ** (-jnp.arange(half, dtype=F32) / half)
    ang = jnp.arange(seq).astype(F32)[:, None] * inv[None, :]
    cos, sin = jnp.cos(ang), jnp.sin(ang)
    zl = lambda c: jnp.zeros((seq, c), F32)
    pad = LANES - dq
    c_tab = jnp.concatenate([zl(MLA_NOPE_DIM), cos, cos, zl(pad)], axis=1)
    s_tab = jnp.concatenate([zl(MLA_NOPE_DIM), -sin, sin, zl(pad)], axis=1)
    scale = dq ** -0.5
    cq_tab = jnp.concatenate([jnp.full((seq, MLA_NOPE_DIM), scale, F32), scale * cos, scale * cos, zl(pad)], axis=1)
    sq_tab = scale * s_tab

    tps = seq // tm
    row = lambda w: pl.BlockSpec((tm, w), lambda i: (i, 0))
    tab = pl.BlockSpec((tm, LANES), lambda i: (i % tps, 0))
    outs = pl.pallas_call(
        functools.partial(_attn_in_body, tiles_per_seq=tps, scan_steps=int(np.log2(tm))),
        out_shape=[jax.ShapeDtypeStruct((n, wf), BF16)] * 3 + [jax.ShapeDtypeStruct((n, LANES), F32)]
        + [jax.ShapeDtypeStruct((n, nh * LANES), BF16)] * 2 + [jax.ShapeDtypeStruct((n, nh * MLA_V_DIM), BF16)],
        grid=(n // tm,),
        in_specs=[row(d), _const_spec((1, d)), _const_spec((d, n1)), _const_spec((1, LANES)),
                  _const_spec((1, MLA_Q_RANK)), _const_spec(wq.shape),
                  _const_spec((1, MLA_KV_RANK)), _const_spec(wkv.shape), tab, tab, tab, tab],
        out_specs=[row(wf)] * 3 + [row(LANES), row(nh * LANES), row(nh * LANES), row(nh * MLA_V_DIM)],
        scratch_shapes=[pltpu.VMEM((1, LANES), F32)],
        compiler_params=_params(("arbitrary",)),
        name="attn_in",
    )(x2, g.reshape(1, d), w1, fb, q_norm.reshape(1, -1), wq, kv_norm.reshape(1, -1), wkv,
      c_tab, s_tab, cq_tab, sq_tab)
    return outs


def _attn_body(*refs, hb, dq, dv, tq, chunk_mask, forget):
    if forget:
        q_ref, k_ref, v_ref, fcol_ref, frow_ref, o_ref = refs
    else:
        q_ref, k_ref, v_ref, o_ref = refs
    seq = q_ref.shape[1]
    r = lax.broadcasted_iota(jnp.int32, (tq, tq), 0)
    c = lax.broadcasted_iota(jnp.int32, (tq, tq), 1)
    mask = (c // CHUNK <= r // CHUNK) if chunk_mask else (c <= r)
    for hh in range(hb):
        ql, vl = slice(hh * dq, (hh + 1) * dq), slice(hh * dv, (hh + 1) * dv)
        for i in range(seq // tq):
            lo, hi = i * tq, (i + 1) * tq
            q = q_ref[0, lo:hi, ql]
            s_d = _dot_nt(q, k_ref[0, lo:hi, ql])
            if forget:
                s_d = s_d + fcol_ref[0, hh, lo:hi, :] - frow_ref[0, hh, :, lo:hi]
            s_d = jnp.where(mask, s_d, NEG)
            m = jnp.max(s_d, axis=-1, keepdims=True)
            if i > 0:
                s_m = _dot_nt(q, k_ref[0, :lo, ql])
                if forget:
                    s_m = s_m + fcol_ref[0, hh, lo:hi, :] - frow_ref[0, hh, :, :lo]
                m = jnp.maximum(m, jnp.max(s_m, axis=-1, keepdims=True))
            p_d = jnp.exp(s_d - m)
            l = jnp.sum(p_d, axis=-1, keepdims=True)
            acc = _dot(p_d.astype(BF16), v_ref[0, lo:hi, vl])
            if i > 0:
                p_m = jnp.exp(s_m - m)
                l = l + jnp.sum(p_m, axis=-1, keepdims=True)
                acc = acc + _dot(p_m.astype(BF16), v_ref[0, :lo, vl])
            o_ref[0, lo:hi, vl] = (acc / l).astype(o_ref.dtype)


def _attention(q, k, v, n_heads, dq, dv, chunk_mask, fcol=None, frow=None):
    b, seq, _ = q.shape
    hb = 2
    tq = min(ATTN_Q_TILE, seq)
    forget = fcol is not None
    blk = lambda w: pl.BlockSpec((1, seq, hb * w), lambda bi, hi: (bi, 0, hi))
    args, specs = [q, k, v], [blk(dq), blk(dq), blk(dv)]
    if forget:
        args += [fcol, frow]
        specs += [pl.BlockSpec((1, hb, seq, 1), lambda bi, hi: (bi, hi, 0, 0)),
                  pl.BlockSpec((1, hb, 1, seq), lambda bi, hi: (bi, hi, 0, 0))]
    return pl.pallas_call(
        functools.partial(_attn_body, hb=hb, dq=dq, dv=dv, tq=tq, chunk_mask=chunk_mask, forget=forget),
        out_shape=jax.ShapeDtypeStruct((b, seq, n_heads * dv), BF16),
        grid=(b, n_heads // hb),
        in_specs=specs,
        out_specs=blk(dv),
        compiler_params=_params(("parallel", "parallel")),
        name="mla_attn" if chunk_mask else "fox_attn",
    )(*args)


def _out_proj_body(*refs, n_in):
    x_ref = refs[0]
    o_ref = refs[-1]
    acc = x_ref[...]
    for j in range(n_in):
        acc = acc + _dot(refs[1 + j][...], refs[1 + n_in + j][...])
    o_ref[...] = acc


def _out_proj(x2, ins, ws):
    n, d = x2.shape
    tm = min(ROW_TILE, n)
    row = lambda w: pl.BlockSpec((tm, w), lambda i: (i, 0))
    return pl.pallas_call(
        functools.partial(_out_proj_body, n_in=len(ins)),
        out_shape=jax.ShapeDtypeStruct((n, d), F32),
        grid=(n // tm,),
        in_specs=[row(d)] + [row(a.shape[1]) for a in ins] + [_const_spec(w.shape) for w in ws],
        out_specs=row(d),
        compiler_params=_params(("parallel",)),
        name="out_proj",
    )(x2, *ins, *[w.astype(BF16) for w in ws])


def _gdn_in_body(x_ref, g_ref, w_ref, alog_ref, dt_ref, qkv_ref, gate_ref, beta_ref, gcum_ref, *, scan_steps):
    wq = 3 * GDN_HEADS * GDN_HEAD_DIM
    wg = GDN_HEADS * GDN_HEAD_DIM
    h = _rms_rows(x_ref[...], g_ref[...]).astype(BF16)
    y = _dot(h, w_ref[...])
    qkv_ref[...] = y[:, :wq]
    gate_ref[...] = y[:, wq:wq + wg]
    small = y[:, wq + wg:]
    beta_ref[...] = jax.nn.sigmoid(small)
    z = small + dt_ref[...]
    softplus = jnp.maximum(z, 0.0) + jnp.log1p(jnp.exp(-jnp.abs(z)))
    g_log = -jnp.exp(alog_ref[...]) * softplus
    gcum_ref[...] = _row_scan(g_log, scan_steps, period=CHUNK)


def _gdn_in(x2, g, w_in, a_log, dt_bias):
    n, d = x2.shape
    tm = min(ROW_TILE, n)
    wq = 3 * GDN_HEADS * GDN_HEAD_DIM
    wg = GDN_HEADS * GDN_HEAD_DIM
    nhd = GDN_HEADS
    small = jnp.concatenate([w_in[:, wq:wq + 2 * nhd], jnp.zeros((d, LANES - 2 * nhd), F32)], axis=1)
    w1 = jnp.concatenate([w_in[:, :wq], w_in[:, wq + 2 * nhd:], small], axis=1).astype(BF16)
    lane_vec = lambda v: jnp.concatenate(
        [jnp.zeros((nhd,), F32), v, jnp.zeros((LANES - 2 * nhd,), F32)]).reshape(1, LANES)
    row = lambda w: pl.BlockSpec((tm, w), lambda i: (i, 0))
    return pl.pallas_call(
        functools.partial(_gdn_in_body, scan_steps=int(np.log2(CHUNK))),
        out_shape=[jax.ShapeDtypeStruct((n, wq), F32), jax.ShapeDtypeStruct((n, wg), F32),
                   jax.ShapeDtypeStruct((n, LANES), F32), jax.ShapeDtypeStruct((n, LANES), F32)],
        grid=(n // tm,),
        in_specs=[row(d), _const_spec((1, d)), _const_spec(w1.shape), _const_spec((1, LANES)),
                  _const_spec((1, LANES))],
        out_specs=[row(wq), row(wg), row(LANES), row(LANES)],
        compiler_params=_params(("parallel",)),
        name="gdn_in",
    )(x2, g.reshape(1, d), w1, lane_vec(a_log), lane_vec(dt_bias))


def _gdn_body(q_ref, k_ref, v_ref, cwq_ref, cwk_ref, cwv_ref, gate_ref, onorm_ref,
              bcol_ref, gcol_ref, grow_ref, o_ref, qs_ref, ks_ref, vs_ref):
    seq = q_ref.shape[1]
    hd = GDN_HEAD_DIM
    rows = lax.broadcasted_iota(jnp.int32, (seq, hd), 0)

    def conv_silu(x_ref, cw_ref):
        x = x_ref[0]
        acc = x * cw_ref[GDN_CONV - 1:GDN_CONV, :]
        for j in range(1, GDN_CONV):
            shifted = jnp.where(rows >= j, pltpu.roll(x, j, axis=0), 0.0)
            acc = acc + shifted * cw_ref[GDN_CONV - 1 - j:GDN_CONV - j, :]
        return acc * jax.nn.sigmoid(acc)

    def l2norm(x):
        return x * lax.rsqrt(jnp.sum(x * x, axis=-1, keepdims=True) + 1e-6)

    qs_ref[...] = l2norm(conv_silu(q_ref, cwq_ref)) * hd ** -0.5
    ks_ref[...] = l2norm(conv_silu(k_ref, cwk_ref))
    vs_ref[...] = conv_silu(v_ref, cwv_ref)

    ri = lax.broadcasted_iota(jnp.int32, (CHUNK, CHUNK), 0)
    ci = lax.broadcasted_iota(jnp.int32, (CHUNK, CHUNK), 1)
    eye = (ri == ci).astype(F32)
    hi_dot = functools.partial(jnp.dot, precision=lax.Precision.HIGHEST, preferred_element_type=F32)

    def chunk_step(ch, state):
        r0 = pl.multiple_of(ch * CHUNK, CHUNK)
        sl = pl.ds(r0, CHUNK)
        q, k, v = qs_ref[sl, :], ks_ref[sl, :], vs_ref[sl, :]
        beta = bcol_ref[0, 0, sl, :]
        g_c = gcol_ref[0, 0, sl, :]
        g_r = grow_ref[0, 0, pl.ds(ch, 1), :]
        g_last = g_r[:, CHUNK - 1:CHUNK]
        k_beta = k * beta
        gram = _dot_nt(jnp.concatenate([k_beta, q], axis=0).astype(BF16), k.astype(BF16))
        decay = jnp.where(ri >= ci, jnp.exp(g_c - g_r), 0.0)
        a = jnp.where(ri > ci, gram[:CHUNK] * decay, 0.0)
        intra = gram[CHUNK:] * decay
        t_mat = eye - a
        a_pow = a
        for _ in range(int(np.log2(CHUNK)) - 1):
            a_pow = hi_dot(a_pow, a_pow)
            t_mat = t_mat + hi_dot(t_mat, a_pow)
        e_g = jnp.exp(g_c)
        uw = _dot(t_mat.astype(BF16), jnp.concatenate([v * beta, k_beta * e_g], axis=1).astype(BF16))
        u, w = uw[:, :hd], uw[:, hd:]
        k_tail = k * jnp.exp(g_last - g_c)
        state_b = state.astype(BF16)
        v_new = u - _dot(w.astype(BF16), state_b)
        v_new_b = v_new.astype(BF16)
        out = _dot((q * e_g).astype(BF16), state_b) + _dot(intra.astype(BF16), v_new_b)
        state = state * jnp.exp(g_last) + _dot(k_tail.T.astype(BF16), v_new_b)
        gate = gate_ref[0, sl, :]
        o_ref[0, sl, :] = (_rms_rows(out, onorm_ref[...]) * (gate * jax.nn.sigmoid(gate))).astype(o_ref.dtype)
        return state

    lax.fori_loop(0, seq // CHUNK, chunk_step, jnp.zeros((hd, hd), F32))


def _gdn(qkv, gate, conv_w, out_norm, bcol, gcol, grow):
    b, seq, _ = qkv.shape
    nh, hd = GDN_HEADS, GDN_HEAD_DIM
    col = lambda off: pl.BlockSpec((1, seq, hd), lambda bi, hi: (bi, 0, hi + off))
    cw = lambda off: pl.BlockSpec((GDN_CONV, hd), lambda bi, hi: (0, hi + off))
    vec = pl.BlockSpec((1, 1, seq, 1), lambda bi, hi: (bi, hi, 0, 0))
    return pl.pallas_call(
        _gdn_body,
        out_shape=jax.ShapeDtypeStruct((b, seq, nh * hd), BF16),
        grid=(b, nh),
        in_specs=[col(0), col(nh), col(2 * nh), cw(0), cw(nh), cw(2 * nh), col(0),
                  pl.BlockSpec((1, hd), lambda bi, hi: (0, 0)), vec, vec,
                  pl.BlockSpec((1, 1, seq // CHUNK, CHUNK), lambda bi, hi: (bi, hi, 0, 0))],
        out_specs=col(0),
        scratch_shapes=[pltpu.VMEM((seq, hd), F32)] * 3,
        compiler_params=_params(("parallel", "parallel")),
        name="gdn",
    )(qkv, qkv, qkv, conv_w, conv_w, conv_w, gate, out_norm.reshape(1, hd), bcol, gcol, grow)


def kernel(x, ffn1_norm, ffn1_w13, ffn1_w2, mix_norm, attn_w_in, fox_f_bias, mla_q_norm, mla_w_uq,
           mla_kv_norm, mla_w_ukv, attn_w_out, gdn_w_in, gdn_conv_w, gdn_a_log, gdn_dt_bias,
           gdn_out_norm, gdn_w_out, ffn2_norm, ffn2_w13, ffn2_w2, final_norm):
    b, seq, d = x.shape
    n = b * seq
    depth = ffn1_norm.shape[0]
    x2 = x.reshape(n, d)
    for layer in range(depth):
        x2 = _ffn(x2, ffn1_norm[layer], ffn1_w13[layer], ffn1_w2[layer])
        i = layer // 2
        if layer % 2 == 0:
            qf, kf, vf, f_cum, qm, km, vm = _attn_in(
                x2, seq, mix_norm[layer], attn_w_in[i], fox_f_bias[i], mla_q_norm[i], mla_w_uq[i],
                mla_kv_norm[i], mla_w_ukv[i])
            f_t = f_cum[:, :FOX_HEADS].reshape(b, seq, FOX_HEADS).transpose(0, 2, 1)
            sh = lambda t: t.reshape(b, seq, -1)
            o_a = _attention(sh(qf), sh(kf), sh(vf), FOX_HEADS, FOX_HEAD_DIM, FOX_HEAD_DIM, False,
                             f_t[:, :, :, None], f_t[:, :, None, :])
            o_b = _attention(sh(qm), sh(km), sh(vm), MLA_HEADS, LANES, MLA_V_DIM, True)
            wf = FOX_HEADS * FOX_HEAD_DIM
            x2 = _out_proj(x2, [o_a.reshape(n, -1), o_b.reshape(n, -1)],
                           [attn_w_out[i][:wf], attn_w_out[i][wf:]])
        else:
            qkv, gate, beta, gcum = _gdn_in(x2, mix_norm[layer], gdn_w_in[i], gdn_a_log[i], gdn_dt_bias[i])
            nh = GDN_HEADS
            heads = lambda t: t.reshape(b, seq, nh).transpose(0, 2, 1)
            b_t, g_t = heads(beta[:, :nh]), heads(gcum[:, nh:2 * nh])
            o = _gdn(qkv.reshape(b, seq, -1), gate.reshape(b, seq, -1), gdn_conv_w[i], gdn_out_norm[i],
                     b_t[..., None], g_t[..., None], g_t.reshape(b, nh, seq // CHUNK, CHUNK))
            x2 = _out_proj(x2, [o.reshape(n, -1)], [gdn_w_out[i]])
        last = layer == depth - 1
        x2 = _ffn(x2, ffn2_norm[layer], ffn2_w13[layer], ffn2_w2[layer], final_norm if last else None)
    return x2.reshape(b, seq, d)
```

```python
import functools

import numpy as np
import jax
import jax.numpy as jnp
from jax import lax
from jax.experimental import pallas as pl
from jax.experimental.pallas import tpu as pltpu

F32 = jnp.float32
BF16 = jnp.bfloat16

CHUNK = 64
RMS_EPS = 1e-6
FOX_HEADS = 8
FOX_HEAD_DIM = 64
MLA_HEADS = 8
MLA_Q_RANK = 256
MLA_KV_RANK = 128
MLA_NOPE_DIM = 64
MLA_ROPE_DIM = 32
MLA_V_DIM = 64
ROPE_THETA = 10000.0
GDN_HEADS = 8
GDN_HEAD_DIM = 128
GDN_CONV = 4

LANES = 128
SUBLANES = 8
VMEM_LIMIT = 56 * 1024 * 1024
ROW_TILE = 512
ATTN_Q_TILE = 256
GDN_HEAD_BLOCK = 4
GDN_PAIRS_PER_STEP = 2
GDN_COL_CHUNK = 512
NEG = float(jnp.finfo(jnp.float32).min)


def _params(sem):
    return pltpu.CompilerParams(dimension_semantics=sem, vmem_limit_bytes=VMEM_LIMIT)


def _rms_rows(x, g):
    return x * lax.rsqrt(jnp.mean(x * x, axis=-1, keepdims=True) + RMS_EPS) * g


def _dot(a, b):
    return jnp.dot(a, b, preferred_element_type=F32)


def _dot_nt(a, b):
    return lax.dot_general(a, b, (((1,), (1,)), ((), ())), preferred_element_type=F32)


def _const_spec(shape):
    return pl.BlockSpec(shape, lambda *_: (0,) * len(shape), pipeline_mode=pl.Buffered(1))


def _row_scan(y, n_steps, period=None):
    rows = lax.broadcasted_iota(jnp.int32, y.shape, 0)
    if period is not None:
        rows = rows % period
    shift = 1
    for _ in range(n_steps):
        y = y + jnp.where(rows >= shift, pltpu.roll(y, shift, axis=0), 0.0)
        shift *= 2
    return y


def _ffn_body(*refs, d_ff, n_chunks, final):
    if final:
        x_ref, g_ref, w13_ref, w2_ref, fg_ref, o_ref, t_ref = refs
    else:
        x_ref, g_ref, w13_ref, w2_ref, o_ref, t_ref = refs
    x = x_ref[...]
    h = _rms_rows(x, g_ref[...]).astype(BF16)
    fc = d_ff // n_chunks
    for j in range(n_chunks):
        a = _dot(h, w13_ref[:, j * fc:(j + 1) * fc])
        b = _dot(h, w13_ref[:, d_ff + j * fc:d_ff + (j + 1) * fc])
        t_ref[:, j * fc:(j + 1) * fc] = (a * jax.nn.sigmoid(a) * b).astype(BF16)
    out = x + 0.5 * _dot(t_ref[...], w2_ref[...])
    if final:
        out = _rms_rows(out, fg_ref[...])
    o_ref[...] = out


def _ffn(x2, g, w13, w2, final_g=None):
    n, d = x2.shape
    d_ff = w2.shape[0]
    tm = min(ROW_TILE, n)
    final = final_g is not None
    args = [x2, g.reshape(1, d), w13.astype(BF16), w2.astype(BF16)]
    specs = [pl.BlockSpec((tm, d), lambda i: (i, 0)), _const_spec((1, d)),
             _const_spec((d, 2 * d_ff)), _const_spec((d_ff, d))]
    if final:
        args.append(final_g.reshape(1, d))
        specs.append(_const_spec((1, d)))
    return pl.pallas_call(
        functools.partial(_ffn_body, d_ff=d_ff, n_chunks=d_ff // 256, final=final),
        out_shape=jax.ShapeDtypeStruct((n, d), F32),
        grid=(n // tm,),
        in_specs=specs,
        out_specs=pl.BlockSpec((tm, d), lambda i: (i, 0)),
        scratch_shapes=[pltpu.VMEM((tm, d_ff), BF16)],
        compiler_params=_params(("parallel",)),
        name="ffn",
    )(*args)


def _attn_in_body(x_ref, g_ref, w_ref, fb_ref, qn_ref, wq_ref, kvn_ref, wkv_ref,
                  ck_ref, sk_ref, cq_ref, sq_ref,
                  qf_ref, kf_ref, vf_ref, fc_ref, qm_ref, km_ref, vm_ref, carry_ref,
                  *, tiles_per_seq, scan_steps):
    wf = FOX_HEADS * FOX_HEAD_DIM
    h = _rms_rows(x_ref[...], g_ref[...]).astype(BF16)
    y = _dot(h, w_ref[...])
    qf_ref[...] = (y[:, :wf] * FOX_HEAD_DIM ** -0.5).astype(BF16)
    kf_ref[...] = y[:, wf:2 * wf].astype(BF16)
    vf_ref[...] = y[:, 2 * wf:3 * wf].astype(BF16)
    o = 3 * wf
    c_q = y[:, o:o + MLA_Q_RANK]
    o += MLA_Q_RANK
    c_kv = y[:, o:o + MLA_KV_RANK]
    o += MLA_KV_RANK
    kp = y[:, o:o + LANES]
    kq = y[:, o + LANES:o + 2 * LANES]
    f_logit = y[:, o + 2 * LANES:o + 3 * LANES]

    z = f_logit + fb_ref[...]
    log_f = jnp.minimum(z, 0.0) - jnp.log1p(jnp.exp(-jnp.abs(z)))

    @pl.when(pl.program_id(0) % tiles_per_seq == 0)
    def _():
        carry_ref[...] = jnp.zeros_like(carry_ref)

    f_cum = _row_scan(log_f, scan_steps) + carry_ref[...]
    fc_ref[...] = f_cum
    carry_ref[...] = f_cum[-1:, :]

    nh = MLA_HEADS
    qb = _dot(_rms_rows(c_q, qn_ref[...]).astype(BF16), wq_ref[...])
    cq = jnp.concatenate([cq_ref[...]] * nh, axis=1)
    sq = jnp.concatenate([sq_ref[...]] * nh, axis=1)
    qm_ref[...] = (qb[:, :nh * LANES] * cq + qb[:, nh * LANES:] * sq).astype(BF16)
    kvb = _dot(_rms_rows(c_kv, kvn_ref[...]).astype(BF16), wkv_ref[...])
    k_rot = kp * ck_ref[...] + kq * sk_ref[...]
    km_ref[...] = (kvb[:, :nh * LANES] + jnp.concatenate([k_rot] * nh, axis=1)).astype(BF16)
    vm_ref[...] = kvb[:, nh * LANES:].astype(BF16)


def _attn_in(x2, seq, g, w_in, f_bias, q_norm, w_uq, kv_norm, w_ukv):
    n, d = x2.shape
    tm = min(ROW_TILE, seq)
    wf = FOX_HEADS * FOX_HEAD_DIM
    half = MLA_ROPE_DIM // 2
    nh = MLA_HEADS

    qa, ka, va, fl, cq, ckv, kpe = jnp.split(
        w_in, np.cumsum([wf, wf, wf, FOX_HEADS, MLA_Q_RANK, MLA_KV_RANK]).tolist(), axis=1)
    z = lambda c: jnp.zeros((d, c), F32)
    p_blk = jnp.concatenate([z(MLA_NOPE_DIM), kpe, z(LANES - MLA_NOPE_DIM - MLA_ROPE_DIM)], axis=1)
    q_blk = jnp.concatenate([z(MLA_NOPE_DIM), kpe[:, half:], kpe[:, :half],
                             z(LANES - MLA_NOPE_DIM - MLA_ROPE_DIM)], axis=1)
    f_blk = jnp.concatenate([fl, z(LANES - FOX_HEADS)], axis=1)
    w1 = jnp.concatenate([qa, ka, va, cq, ckv, p_blk, q_blk, f_blk], axis=1).astype(BF16)
    n1 = w1.shape[1]
    fb = jnp.concatenate([f_bias, jnp.zeros((LANES - FOX_HEADS,), F32)]).reshape(1, LANES)

    dq = MLA_NOPE_DIM + MLA_ROPE_DIM
    wq3 = w_uq.reshape(MLA_Q_RANK, nh, dq)
    zq = jnp.zeros((MLA_Q_RANK, nh, LANES - dq), F32)
    wa = jnp.concatenate([wq3, zq], axis=2).reshape(MLA_Q_RANK, nh * LANES)
    wb = jnp.concatenate([jnp.zeros((MLA_Q_RANK, nh, MLA_NOPE_DIM), F32),
                          wq3[:, :, MLA_NOPE_DIM + half:], wq3[:, :, MLA_NOPE_DIM:MLA_NOPE_DIM + half],
                          zq], axis=2).reshape(MLA_Q_RANK, nh * LANES)
    wq = jnp.concatenate([wa, wb], axis=1).astype(BF16)
    wkv3 = w_ukv.reshape(MLA_KV_RANK, nh, MLA_NOPE_DIM + MLA_V_DIM)
    wk = jnp.concatenate([wkv3[:, :, :MLA_NOPE_DIM],
                          jnp.zeros((MLA_KV_RANK, nh, LANES - MLA_NOPE_DIM), F32)], axis=2)
    wkv = jnp.concatenate([wk.reshape(MLA_KV_RANK, nh * LANES),
                           wkv3[:, :, MLA_NOPE_DIM:].reshape(MLA_KV_RANK, nh * MLA_V_DIM)],
                          axis=1).astype(BF16)

    inv = ROPE_THETA ** (-jnp.arange(half, dtype=F32) / half)
    ang = jnp.arange(seq).astype(F32)[:, None] * inv[None, :]
    cos, sin = jnp.cos(ang), jnp.sin(ang)
    zl = lambda c: jnp.zeros((seq, c), F32)
    pad = LANES - dq
    c_tab = jnp.concatenate([zl(MLA_NOPE_DIM), cos, cos, zl(pad)], axis=1)
    s_tab = jnp.concatenate([zl(MLA_NOPE_DIM), -sin, sin, zl(pad)], axis=1)
    scale = dq ** -0.5
    cq_tab = jnp.concatenate([jnp.full((seq, MLA_NOPE_DIM), scale, F32), scale * cos, scale * cos, zl(pad)], axis=1)
    sq_tab = scale * s_tab

    tps = seq // tm
    row = lambda w: pl.BlockSpec((tm, w), lambda i: (i, 0))
    tab = pl.BlockSpec((tm, LANES), lambda i: (i % tps, 0))
    outs = pl.pallas_call(
        functools.partial(_attn_in_body, tiles_per_seq=tps, scan_steps=int(np.log2(tm))),
        out_shape=[jax.ShapeDtypeStruct((n, wf), BF16)] * 3 + [jax.ShapeDtypeStruct((n, LANES), F32)]
        + [jax.ShapeDtypeStruct((n, nh * LANES), BF16)] * 2 + [jax.ShapeDtypeStruct((n, nh * MLA_V_DIM), BF16)],
        grid=(n // tm,),
        in_specs=[row(d), _const_spec((1, d)), _const_spec((d, n1)), _const_spec((1, LANES)),
                  _const_spec((1, MLA_Q_RANK)), _const_spec(wq.shape),
                  _const_spec((1, MLA_KV_RANK)), _const_spec(wkv.shape), tab, tab, tab, tab],
        out_specs=[row(wf)] * 3 + [row(LANES), row(nh * LANES), row(nh * LANES), row(nh * MLA_V_DIM)],
        scratch_shapes=[pltpu.VMEM((1, LANES), F32)],
        compiler_params=_params(("arbitrary",)),
        name="attn_in",
    )(x2, g.reshape(1, d), w1, fb, q_norm.reshape(1, -1), wq, kv_norm.reshape(1, -1), wkv,
      c_tab, s_tab, cq_tab, sq_tab)
    return outs


def _attn_body(*refs, hb, dq, dv, tq, chunk_mask, forget):
    if forget:
        q_ref, k_ref, v_ref, fcol_ref, frow_ref, o_ref = refs
    else:
        q_ref, k_ref, v_ref, o_ref = refs
    seq = q_ref.shape[1]
    r = lax.broadcasted_iota(jnp.int32, (tq, tq), 0)
    c = lax.broadcasted_iota(jnp.int32, (tq, tq), 1)
    mask = (c // CHUNK <= r // CHUNK) if chunk_mask else (c <= r)
    for hh in range(hb):
        ql, vl = slice(hh * dq, (hh + 1) * dq), slice(hh * dv, (hh + 1) * dv)
        for i in range(seq // tq):
            lo, hi = i * tq, (i + 1) * tq
            q = q_ref[0, lo:hi, ql]
            s_d = _dot_nt(q, k_ref[0, lo:hi, ql])
            if forget:
                s_d = s_d + fcol_ref[0, hh, lo:hi, :] - frow_ref[0, hh, :, lo:hi]
            s_d = jnp.where(mask, s_d, NEG)
            m = jnp.max(s_d, axis=-1, keepdims=True)
            if i > 0:
                s_m = _dot_nt(q, k_ref[0, :lo, ql])
                if forget:
                    s_m = s_m + fcol_ref[0, hh, lo:hi, :] - frow_ref[0, hh, :, :lo]
                m = jnp.maximum(m, jnp.max(s_m, axis=-1, keepdims=True))
            p_d = jnp.exp(s_d - m)
            l = jnp.sum(p_d, axis=-1, keepdims=True)
            acc = _dot(p_d.astype(BF16), v_ref[0, lo:hi, vl])
            if i > 0:
                p_m = jnp.exp(s_m - m)
                l = l + jnp.sum(p_m, axis=-1, keepdims=True)
                acc = acc + _dot(p_m.astype(BF16), v_ref[0, :lo, vl])
            o_ref[0, lo:hi, vl] = (acc / l).astype(o_ref.dtype)


def _attention(q, k, v, n_heads, dq, dv, chunk_mask, fcol=None, frow=None):
    b, seq, _ = q.shape
    hb = 2
    tq = min(ATTN_Q_TILE, seq)
    forget = fcol is not None
    blk = lambda w: pl.BlockSpec((1, seq, hb * w), lambda bi, hi: (bi, 0, hi))
    args, specs = [q, k, v], [blk(dq), blk(dq), blk(dv)]
    if forget:
        args += [fcol, frow]
        specs += [pl.BlockSpec((1, hb, seq, 1), lambda bi, hi: (bi, hi, 0, 0)),
                  pl.BlockSpec((1, hb, 1, seq), lambda bi, hi: (bi, hi, 0, 0))]
    return pl.pallas_call(
        functools.partial(_attn_body, hb=hb, dq=dq, dv=dv, tq=tq, chunk_mask=chunk_mask, forget=forget),
        out_shape=jax.ShapeDtypeStruct((b, seq, n_heads * dv), BF16),
        grid=(b, n_heads // hb),
        in_specs=specs,
        out_specs=blk(dv),
        compiler_params=_params(("parallel", "parallel")),
        name="mla_attn" if chunk_mask else "fox_attn",
    )(*args)


def _out_proj_body(*refs, n_in):
    x_ref = refs[0]
    o_ref = refs[-1]
    acc = x_ref[...]
    for j in range(n_in):
        acc = acc + _dot(refs[1 + j][...], refs[1 + n_in + j][...])
    o_ref[...] = acc


def _out_proj(x2, ins, ws):
    n, d = x2.shape
    tm = min(ROW_TILE, n)
    row = lambda w: pl.BlockSpec((tm, w), lambda i: (i, 0))
    return pl.pallas_call(
        functools.partial(_out_proj_body, n_in=len(ins)),
        out_shape=jax.ShapeDtypeStruct((n, d), F32),
        grid=(n // tm,),
        in_specs=[row(d)] + [row(a.shape[1]) for a in ins] + [_const_spec(w.shape) for w in ws],
        out_specs=row(d),
        compiler_params=_params(("parallel",)),
        name="out_proj",
    )(x2, *ins, *[w.astype(BF16) for w in ws])


def _gdn_in_body(x_ref, g_ref, w_ref, cw_ref, alog_ref, dt_ref,
                 q_ref, k_ref, kt_ref, v_ref, sg_ref, beta_ref, gcum_ref, carry_ref,
                 *, tiles_per_seq, scan_steps, cc):
    hd = GDN_HEAD_DIM
    width = GDN_HEADS * hd
    tm = x_ref.shape[0]
    h = _rms_rows(x_ref[...], g_ref[...]).astype(BF16)

    @pl.when(pl.program_id(0) % tiles_per_seq == 0)
    def _():
        carry_ref[...] = jnp.zeros_like(carry_ref)

    top_rows = lax.broadcasted_iota(jnp.int32, (SUBLANES, cc), 0)
    for part, out_ref in enumerate((q_ref, k_ref, v_ref)):
        for c in range(width // cc):
            cols = slice(part * width + c * cc, part * width + (c + 1) * cc)
            y = _dot(h, w_ref[:, cols])
            prev = carry_ref[:, cols]
            acc = y * cw_ref[GDN_CONV - 1:GDN_CONV, cols]
            for j in range(1, GDN_CONV):
                rolled = pltpu.roll(y, j, axis=0)
                top = jnp.where(top_rows < j, pltpu.roll(prev, j, axis=0), rolled[:SUBLANES])
                shifted = jnp.concatenate([top, rolled[SUBLANES:]], axis=0)
                acc = acc + shifted * cw_ref[GDN_CONV - 1 - j:GDN_CONV - j, cols]
            carry_ref[:, cols] = y[tm - SUBLANES:]
            a = acc * jax.nn.sigmoid(acc)
            if part < 2:
                heads = []
                for hh in range(cc // hd):
                    xh = a[:, hh * hd:(hh + 1) * hd]
                    xh = xh * lax.rsqrt(jnp.sum(xh * xh, axis=-1, keepdims=True) + 1e-6)
                    heads.append(xh * hd ** -0.5 if part == 0 else xh)
                a = jnp.concatenate(heads, axis=1)
            out_ref[:, c * cc:(c + 1) * cc] = a.astype(BF16)
            if part == 1:
                kt_ref[c * cc:(c + 1) * cc, :] = a.T.astype(BF16)
    for c in range(width // cc):
        gate = _dot(h, w_ref[:, 3 * width + c * cc:3 * width + (c + 1) * cc])
        sg_ref[:, c * cc:(c + 1) * cc] = (gate * jax.nn.sigmoid(gate)).astype(BF16)
    small = _dot(h, w_ref[:, 4 * width:])
    beta_ref[...] = jax.nn.sigmoid(small)
    z = small + dt_ref[...]
    softplus = jnp.maximum(z, 0.0) + jnp.log1p(jnp.exp(-jnp.abs(z)))
    g_log = -jnp.exp(alog_ref[...]) * softplus
    gcum_ref[...] = _row_scan(g_log, scan_steps, period=CHUNK)


def _gdn_in(x2, seq, g, w_in, conv_w, a_log, dt_bias):
    n, d = x2.shape
    tm = min(ROW_TILE, seq)
    nhd = GDN_HEADS
    width = nhd * GDN_HEAD_DIM
    cc = min(GDN_COL_CHUNK, width)
    small = jnp.concatenate([w_in[:, 3 * width:3 * width + 2 * nhd], jnp.zeros((d, LANES - 2 * nhd), F32)], axis=1)
    w1 = jnp.concatenate([w_in[:, :3 * width], w_in[:, 3 * width + 2 * nhd:], small], axis=1).astype(BF16)
    lane_vec = lambda v: jnp.concatenate(
        [jnp.zeros((nhd,), F32), v, jnp.zeros((LANES - 2 * nhd,), F32)]).reshape(1, LANES)
    row = lambda w, dt: (jax.ShapeDtypeStruct((n, w), dt), pl.BlockSpec((tm, w), lambda i: (i, 0)))
    outs = [row(width, BF16), row(width, BF16),
            (jax.ShapeDtypeStruct((width, n), BF16), pl.BlockSpec((width, tm), lambda i: (0, i))),
            row(width, BF16), row(width, BF16), row(LANES, F32), row(LANES, F32)]
    return pl.pallas_call(
        functools.partial(_gdn_in_body, tiles_per_seq=seq // tm, scan_steps=int(np.log2(CHUNK)), cc=cc),
        out_shape=[o[0] for o in outs],
        grid=(n // tm,),
        in_specs=[pl.BlockSpec((tm, d), lambda i: (i, 0)), _const_spec((1, d)), _const_spec(w1.shape),
                  _const_spec(conv_w.shape), _const_spec((1, LANES)), _const_spec((1, LANES))],
        out_specs=[o[1] for o in outs],
        scratch_shapes=[pltpu.VMEM((SUBLANES, 3 * width), F32)],
        compiler_params=_params(("arbitrary",)),
        name="gdn_in",
    )(x2, g.reshape(1, d), w1, conv_w, lane_vec(a_log), lane_vec(dt_bias))


def _gdn_body(q_ref, k_ref, kt_ref, v_ref, sg_ref, onorm_ref, g_ref, b_ref, gle_ref, glo_ref,
              o_ref, mq_ref, no_ref, *, hb, pairs_per_step):
    seq = q_ref.shape[1]
    hd = GDN_HEAD_DIM
    pair = 2 * CHUNK
    n_pairs = seq // pair
    ri = lax.broadcasted_iota(jnp.int32, (CHUNK, pair), 0)
    li = lax.broadcasted_iota(jnp.int32, (CHUNK, pair), 1)
    cj = li % CHUNK
    lane_lo = li < CHUNK
    row_lo = lax.broadcasted_iota(jnp.int32, (pair, pair), 0) < CHUNK
    eye = (ri == cj).astype(F32)

    def pack(x):
        return jnp.where(lane_lo, x[:CHUNK], x[CHUNK:])

    def block_diag(x):
        return jnp.concatenate([jnp.where(lane_lo, x, 0.0), jnp.where(lane_lo, 0.0, x)], axis=0).astype(BF16)

    def pair_math(loaded):
        heads = range(len(loaded))
        k2, q2, v2, kt2, g_row, b_row, gl_row = zip(*loaded)
        g_col = [jnp.broadcast_to(g_row[h], (pair, pair)).T for h in heads]
        b_col = [jnp.broadcast_to(b_row[h], (pair, pair)).T for h in heads]
        decay = [jnp.where(ri >= cj, jnp.exp(pack(g_col[h]) - g_row[h]), 0.0) for h in heads]
        gram = [_dot(jnp.concatenate([k2[h][:CHUNK], q2[h][:CHUNK], k2[h][CHUNK:], q2[h][CHUNK:]], axis=0),
                     kt2[h]) for h in heads]
        kk = [jnp.where(lane_lo, gram[h][:CHUNK], gram[h][2 * CHUNK:3 * CHUNK]) for h in heads]
        qk = [jnp.where(lane_lo, gram[h][CHUNK:2 * CHUNK], gram[h][3 * CHUNK:]) for h in heads]
        pw = [jnp.where(ri > cj, -(kk[h] * pack(b_col[h]) * decay[h]), 0.0) for h in heads]
        intra = [qk[h] * decay[h] for h in heads]
        t_mat = [eye + pw[h] for h in heads]
        n_sq = int(np.log2(CHUNK))
        for it in range(n_sq):
            rhs = [block_diag(pw[h]) for h in heads]
            if it == 0:
                pw = [_dot(pw[h].astype(BF16), rhs[h]) for h in heads]
            elif it < n_sq - 1:
                res = [_dot(jnp.concatenate([pw[h], t_mat[h]], axis=0).astype(BF16), rhs[h]) for h in heads]
                pw = [res[h][:CHUNK] for h in heads]
                t_mat = [t_mat[h] + res[h][CHUNK:] for h in heads]
            else:
                t_mat = [t_mat[h] + _dot(t_mat[h].astype(BF16), rhs[h]) for h in heads]
        ub = [_dot(block_diag(t_mat[h] * b_row[h]), v2[h]).astype(BF16) for h in heads]
        wb = [_dot(block_diag(t_mat[h] * (b_row[h] * jnp.exp(g_row[h]))), k2[h]).astype(BF16) for h in heads]
        kt_t = [(kt2[h].astype(F32) * jnp.exp(gl_row[h] - g_row[h])).astype(BF16) for h in heads]
        zero = jnp.zeros_like(wb[0])
        rhs4 = [jnp.concatenate([jnp.where(row_lo, wb[h], zero), jnp.where(row_lo, ub[h], zero),
                                 jnp.where(row_lo, zero, wb[h]), jnp.where(row_lo, zero, ub[h])], axis=1)
                for h in heads]
        mn = [_dot(kt_t[h], rhs4[h]) for h in heads]
        iwu = [_dot(block_diag(intra[h]), jnp.concatenate([wb[h], ub[h]], axis=1)) for h in heads]
        q_eff = [q2[h].astype(F32) * jnp.exp(g_col[h]) - iwu[h][:, :hd] for h in heads]
        return list(zip(mn, iwu, q_eff))

    def pair_step(it, carry):
        items = [(it * pairs_per_step + j, hh) for j in range(pairs_per_step) for hh in range(hb)]
        loaded = []
        for p, hh in items:
            rows = pl.ds(pl.multiple_of(p * pair, pair), pair)
            vec = pl.ds(p, 1)
            hc = slice(hh * hd, (hh + 1) * hd)
            gl_row = jnp.where(lane_lo[:1], gle_ref[0, hh, vec, :], glo_ref[0, hh, vec, :])
            loaded.append((k_ref[0, rows, hc], q_ref[0, rows, hc], v_ref[0, rows, hc], kt_ref[hc, rows],
                           g_ref[0, hh, vec, :], b_ref[0, hh, vec, :], gl_row))
        results = pair_math(loaded)
        for (p, hh), (mn, iwu, q_eff) in zip(items, results):
            for half in range(2):
                ch = 2 * p + half
                rs = slice(half * CHUNK, (half + 1) * CHUNK)
                mq_ref[hh, ch, :hd, :] = mn[:, 2 * half * hd:(2 * half + 1) * hd].astype(BF16)
                mq_ref[hh, ch, hd:, :] = q_eff[rs].astype(BF16)
                no_ref[hh, ch, :hd, :] = mn[:, (2 * half + 1) * hd:(2 * half + 2) * hd]
                no_ref[hh, ch, hd:, :] = iwu[rs, hd:]
        return carry

    lax.fori_loop(0, n_pairs // pairs_per_step, pair_step, 0)

    def scan_step(p, states):
        states = list(states)
        for half, gl_ref in enumerate((gle_ref, glo_ref)):
            ch = 2 * p + half
            rows = pl.ds(pl.multiple_of(ch * CHUNK, CHUNK), CHUNK)
            res = [_dot(mq_ref[hh, ch], states[hh].astype(BF16)) for hh in range(hb)]
            outs = []
            for hh in range(hb):
                outs.append(res[hh][hd:] + no_ref[hh, ch, hd:, :])
                states[hh] = (states[hh] * jnp.exp(gl_ref[0, hh, pl.ds(p, 1), :]) - res[hh][:hd]
                              + no_ref[hh, ch, :hd, :])
            for hh in range(hb):
                hc = slice(hh * hd, (hh + 1) * hd)
                o_ref[0, rows, hc] = (_rms_rows(outs[hh], onorm_ref[...])
                                      * sg_ref[0, rows, hc].astype(F32)).astype(o_ref.dtype)
        return tuple(states)

    lax.fori_loop(0, n_pairs, scan_step, tuple(jnp.zeros((hd, hd), F32) for _ in range(hb)))


def _gdn(q, k, kt, v, sg, out_norm, g_t, b_t):
    b, seq, _ = q.shape
    nh, hd = GDN_HEADS, GDN_HEAD_DIM
    hb = GDN_HEAD_BLOCK
    pair = 2 * CHUNK
    n_pairs = seq // pair
    g_rows = g_t.reshape(b, nh, n_pairs, pair)
    b_rows = b_t.reshape(b, nh, n_pairs, pair)
    g_last = g_t.reshape(b, nh, n_pairs, 2, CHUNK)[..., CHUNK - 1]
    gl_even = jnp.broadcast_to(g_last[..., 0:1], (b, nh, n_pairs, pair))
    gl_odd = jnp.broadcast_to(g_last[..., 1:2], (b, nh, n_pairs, pair))
    tok = pl.BlockSpec((1, seq, hb * hd), lambda bi, hi: (bi, 0, hi))
    vec = pl.BlockSpec((1, hb, n_pairs, pair), lambda bi, hi: (bi, hi, 0, 0))
    return pl.pallas_call(
        functools.partial(_gdn_body, hb=hb, pairs_per_step=GDN_PAIRS_PER_STEP),
        out_shape=jax.ShapeDtypeStruct((b, seq, nh * hd), BF16),
        grid=(b, nh // hb),
        in_specs=[tok, tok, pl.BlockSpec((hb * hd, seq), lambda bi, hi: (hi, bi)), tok, tok,
                  pl.BlockSpec((1, hd), lambda bi, hi: (0, 0)), vec, vec, vec, vec],
        out_specs=tok,
        scratch_shapes=[pltpu.VMEM((hb, seq // CHUNK, hd + CHUNK, hd), BF16),
                        pltpu.VMEM((hb, seq // CHUNK, hd + CHUNK, hd), F32)],
        compiler_params=_params(("parallel", "parallel")),
        name="gdn",
    )(q, k, kt, v, sg, out_norm.reshape(1, hd), g_rows, b_rows, gl_even, gl_odd)


def kernel(x, ffn1_norm, ffn1_w13, ffn1_w2, mix_norm, attn_w_in, fox_f_bias, mla_q_norm, mla_w_uq,
           mla_kv_norm, mla_w_ukv, attn_w_out, gdn_w_in, gdn_conv_w, gdn_a_log, gdn_dt_bias,
           gdn_out_norm, gdn_w_out, ffn2_norm, ffn2_w13, ffn2_w2, final_norm):
    b, seq, d = x.shape
    n = b * seq
    depth = ffn1_norm.shape[0]
    x2 = x.reshape(n, d)
    for layer in range(depth):
        x2 = _ffn(x2, ffn1_norm[layer], ffn1_w13[layer], ffn1_w2[layer])
        i = layer // 2
        if layer % 2 == 0:
            qf, kf, vf, f_cum, qm, km, vm = _attn_in(
                x2, seq, mix_norm[layer], attn_w_in[i], fox_f_bias[i], mla_q_norm[i], mla_w_uq[i],
                mla_kv_norm[i], mla_w_ukv[i])
            f_t = f_cum[:, :FOX_HEADS].reshape(b, seq, FOX_HEADS).transpose(0, 2, 1)
            sh = lambda t: t.reshape(b, seq, -1)
            o_a = _attention(sh(qf), sh(kf), sh(vf), FOX_HEADS, FOX_HEAD_DIM, FOX_HEAD_DIM, False,
                             f_t[:, :, :, None], f_t[:, :, None, :])
            o_b = _attention(sh(qm), sh(km), sh(vm), MLA_HEADS, LANES, MLA_V_DIM, True)
            wf = FOX_HEADS * FOX_HEAD_DIM
            x2 = _out_proj(x2, [o_a.reshape(n, -1), o_b.reshape(n, -1)],
                           [attn_w_out[i][:wf], attn_w_out[i][wf:]])
        else:
            q, k, kt, v, sg, beta, gcum = _gdn_in(x2, seq, mix_norm[layer], gdn_w_in[i], gdn_conv_w[i],
                                                  gdn_a_log[i], gdn_dt_bias[i])
            nh = GDN_HEADS
            heads = lambda t: t.reshape(b, seq, nh).transpose(0, 2, 1)
            sh = lambda t: t.reshape(b, seq, -1)
            o = _gdn(sh(q), sh(k), kt, sh(v), sh(sg), gdn_out_norm[i],
                     heads(gcum[:, nh:2 * nh]), heads(beta[:, :nh]))
            x2 = _out_proj(x2, [o.reshape(n, -1)], [gdn_w_out[i]])
        last = layer == depth - 1
        x2 = _ffn(x2, ffn2_norm[layer], ffn2_w13[layer], ffn2_w2[layer], final_norm if last else None)
    return x2.reshape(b, seq, d)
```

```python
import functools

import numpy as np
import jax
import jax.numpy as jnp
from jax import lax
from jax.experimental import pallas as pl
from jax.experimental.pallas import tpu as pltpu

F32 = jnp.float32
BF16 = jnp.bfloat16

CHUNK = 64
RMS_EPS = 1e-6
FOX_HEADS = 8
FOX_HEAD_DIM = 64
MLA_HEADS = 8
MLA_Q_RANK = 256
MLA_KV_RANK = 128
MLA_NOPE_DIM = 64
MLA_ROPE_DIM = 32
MLA_V_DIM = 64
ROPE_THETA = 10000.0
GDN_HEADS = 8
GDN_HEAD_DIM = 128
GDN_CONV = 4

LANES = 128
SUBLANES = 8
VMEM_LIMIT = 56 * 1024 * 1024
ROW_TILE = 512
ATTN_Q_TILE = 256
ATTN_PIPELINE_DEPTH = 4
GDN_HEAD_BLOCK = 4
GDN_PAIRS_PER_STEP = 2
GDN_COL_CHUNK = 512
NEG = float(jnp.finfo(jnp.float32).min)
LOG2E = float(np.log2(np.e))
N_FORGET_PIECES = 3


def _params(sem):
    return pltpu.CompilerParams(dimension_semantics=sem, vmem_limit_bytes=VMEM_LIMIT)


def _rms_rows(x, g):
    return x * lax.rsqrt(jnp.mean(x * x, axis=-1, keepdims=True) + RMS_EPS) * g


def _dot(a, b):
    return jnp.dot(a, b, preferred_element_type=F32)


def _dot_nt(a, b):
    return lax.dot_general(a, b, (((1,), (1,)), ((), ())), preferred_element_type=F32)


def _const_spec(shape):
    return pl.BlockSpec(shape, lambda *_: (0,) * len(shape), pipeline_mode=pl.Buffered(1))


def _row_scan(y, n_steps, period=None):
    rows = lax.broadcasted_iota(jnp.int32, y.shape, 0)
    if period is not None:
        rows = rows % period
    shift = 1
    for _ in range(n_steps):
        y = y + jnp.where(rows >= shift, pltpu.roll(y, shift, axis=0), 0.0)
        shift *= 2
    return y


def _ffn_body(*refs, d_ff, n_chunks, n_mix, final):
    x_ref, refs = refs[0], refs[1:]
    mix_refs, wo_refs, refs = refs[:n_mix], refs[n_mix:2 * n_mix], refs[2 * n_mix:]
    g_ref, w13_ref, w2_ref = refs[:3]
    fg_ref = refs[3] if final else None
    o_ref, t_ref = refs[-2:]
    x = x_ref[...]
    for m_ref, wo_ref in zip(mix_refs, wo_refs):
        x = x + _dot(m_ref[...], wo_ref[...])
    h = _rms_rows(x, g_ref[...]).astype(BF16)
    fc = d_ff // n_chunks
    for j in range(n_chunks):
        a = _dot(h, w13_ref[:, j * fc:(j + 1) * fc])
        b = _dot(h, w13_ref[:, d_ff + j * fc:d_ff + (j + 1) * fc])
        t_ref[:, j * fc:(j + 1) * fc] = (a * jax.nn.sigmoid(a) * b).astype(BF16)
    out = x + 0.5 * _dot(t_ref[...], w2_ref[...])
    if final:
        out = _rms_rows(out, fg_ref[...])
    o_ref[...] = out


def _ffn(x2, layer, g, w13, w2, mix=(), w_out=(), final_g=None):
    n, d = x2.shape
    d_ff = w2.shape[1]
    tm = min(ROW_TILE, n)
    final = final_g is not None
    row = lambda w: pl.BlockSpec((tm, w), lambda i: (i, 0))
    layer_spec = lambda *shape: pl.BlockSpec((None,) + shape, lambda i: (layer,) + (0,) * len(shape),
                                             pipeline_mode=pl.Buffered(1))
    args = [x2, *mix, *w_out, g.reshape(-1, 1, d), w13, w2]
    specs = ([row(d)] + [row(m.shape[1]) for m in mix] + [_const_spec(w.shape) for w in w_out]
             + [layer_spec(1, d), layer_spec(d, 2 * d_ff), layer_spec(d_ff, d)])
    if final:
        args.append(final_g.reshape(1, d))
        specs.append(_const_spec((1, d)))
    return pl.pallas_call(
        functools.partial(_ffn_body, d_ff=d_ff, n_chunks=d_ff // 256, n_mix=len(mix), final=final),
        out_shape=jax.ShapeDtypeStruct((n, d), F32),
        grid=(n // tm,),
        in_specs=specs,
        out_specs=row(d),
        scratch_shapes=[pltpu.VMEM((tm, d_ff), BF16)],
        compiler_params=_params(("parallel",)),
        name="ffn",
    )(*args)


def _attn_in_body(x_ref, g_ref, w_ref, fb_ref, qn_ref, wq_ref, kvn_ref, wkv_ref,
                  ck_ref, sk_ref, cq_ref, sq_ref, place_ref,
                  qf_ref, kf_ref, vf_ref, qm_ref, km_ref, vm_ref, carry_ref,
                  *, tiles_per_seq, scan_steps):
    wf = FOX_HEADS * FOX_HEAD_DIM
    hdim = FOX_HEAD_DIM
    tm = x_ref.shape[0]
    h = _rms_rows(x_ref[...], g_ref[...]).astype(BF16)
    y = _dot(h, w_ref[...])
    q_t = (y[:, :wf] * (hdim ** -0.5 * LOG2E)).T
    spare = jnp.where(lax.broadcasted_iota(jnp.int32, (LANES - hdim, tm), 0) < N_FORGET_PIECES, -1.0, 0.0)
    qf_ref[...] = jnp.concatenate(
        [blk for hh in range(FOX_HEADS) for blk in (q_t[hh * hdim:(hh + 1) * hdim], spare)], axis=0).astype(BF16)
    vf_ref[...] = y[:, 2 * wf:3 * wf].T.astype(BF16)
    o = 3 * wf
    c_q = y[:, o:o + MLA_Q_RANK]
    o += MLA_Q_RANK
    c_kv = y[:, o:o + MLA_KV_RANK]
    o += MLA_KV_RANK
    kp = y[:, o:o + LANES]
    kq = y[:, o + LANES:o + 2 * LANES]
    f_logit = y[:, o + 2 * LANES:o + 3 * LANES]

    z = f_logit + fb_ref[...]
    log_f = jnp.minimum(z, 0.0) - jnp.log1p(jnp.exp(-jnp.abs(z)))

    @pl.when(pl.program_id(0) % tiles_per_seq == 0)
    def _():
        carry_ref[...] = jnp.zeros_like(carry_ref)

    f_cum = _row_scan(log_f, scan_steps) + carry_ref[...]
    carry_ref[...] = f_cum[-1:, :]
    f_hi = (f_cum * LOG2E).astype(BF16)
    r1 = f_cum * LOG2E - f_hi.astype(F32)
    f_mid = r1.astype(BF16)
    f_lo = (r1 - f_mid.astype(F32)).astype(BF16)
    placed = _dot(jnp.concatenate([f_hi, f_mid, f_lo], axis=1), place_ref[...])
    low = lax.broadcasted_iota(jnp.int32, (tm, LANES), 1) < hdim
    for j in range(wf // LANES):
        kk = y[:, wf + j * LANES:wf + (j + 1) * LANES]
        for hh, src in enumerate((kk, pltpu.roll(kk, hdim, axis=1))):
            cols = slice((2 * j + hh) * LANES, (2 * j + hh + 1) * LANES)
            kf_ref[:, cols] = jnp.where(low, src, placed[:, cols]).astype(BF16)

    nh = MLA_HEADS
    qb = _dot(_rms_rows(c_q, qn_ref[...]).astype(BF16), wq_ref[...])
    cq = jnp.concatenate([cq_ref[...]] * nh, axis=1)
    sq = jnp.concatenate([sq_ref[...]] * nh, axis=1)
    qm_ref[...] = (qb[:, :nh * LANES] * cq + qb[:, nh * LANES:] * sq).T.astype(BF16)
    kvb = _dot(_rms_rows(c_kv, kvn_ref[...]).astype(BF16), wkv_ref[...])
    k_rot = kp * ck_ref[...] + kq * sk_ref[...]
    km_ref[...] = (kvb[:, :nh * LANES] + jnp.concatenate([k_rot] * nh, axis=1)).astype(BF16)
    vm_ref[...] = kvb[:, nh * LANES:].T.astype(BF16)


def _attn_in(x2, seq, g, w_in, f_bias, q_norm, w_uq, kv_norm, w_ukv):
    n, d = x2.shape
    tm = min(ROW_TILE, seq)
    wf = FOX_HEADS * FOX_HEAD_DIM
    half = MLA_ROPE_DIM // 2
    nh = MLA_HEADS

    qa, ka, va, fl, cq, ckv, kpe = jnp.split(
        w_in, np.cumsum([wf, wf, wf, FOX_HEADS, MLA_Q_RANK, MLA_KV_RANK]).tolist(), axis=1)
    z = lambda c: jnp.zeros((d, c), F32)
    p_blk = jnp.concatenate([z(MLA_NOPE_DIM), kpe, z(LANES - MLA_NOPE_DIM - MLA_ROPE_DIM)], axis=1)
    q_blk = jnp.concatenate([z(MLA_NOPE_DIM), kpe[:, half:], kpe[:, :half],
                             z(LANES - MLA_NOPE_DIM - MLA_ROPE_DIM)], axis=1)
    f_blk = jnp.concatenate([fl, z(LANES - FOX_HEADS)], axis=1)
    w1 = jnp.concatenate([qa, ka, va, cq, ckv, p_blk, q_blk, f_blk], axis=1).astype(BF16)
    n1 = w1.shape[1]
    fb = jnp.concatenate([f_bias, jnp.zeros((LANES - FOX_HEADS,), F32)]).reshape(1, LANES)
    n_piece = N_FORGET_PIECES
    src = (np.arange(n_piece)[:, None] * LANES + np.arange(FOX_HEADS)[None, :]).ravel()
    dst = (np.arange(FOX_HEADS)[None, :] * LANES + FOX_HEAD_DIM + np.arange(n_piece)[:, None]).ravel()
    place_np = np.zeros((n_piece * LANES, FOX_HEADS * LANES), np.float32)
    place_np[src, dst] = 1.0
    place = jnp.asarray(place_np, BF16)

    dq = MLA_NOPE_DIM + MLA_ROPE_DIM
    wq3 = w_uq.reshape(MLA_Q_RANK, nh, dq)
    zq = jnp.zeros((MLA_Q_RANK, nh, LANES - dq), F32)
    wa = jnp.concatenate([wq3, zq], axis=2).reshape(MLA_Q_RANK, nh * LANES)
    wb = jnp.concatenate([jnp.zeros((MLA_Q_RANK, nh, MLA_NOPE_DIM), F32),
                          wq3[:, :, MLA_NOPE_DIM + half:], wq3[:, :, MLA_NOPE_DIM:MLA_NOPE_DIM + half],
                          zq], axis=2).reshape(MLA_Q_RANK, nh * LANES)
    wq = jnp.concatenate([wa, wb], axis=1).astype(BF16)
    wkv3 = w_ukv.reshape(MLA_KV_RANK, nh, MLA_NOPE_DIM + MLA_V_DIM)
    wk = jnp.concatenate([wkv3[:, :, :MLA_NOPE_DIM],
                          jnp.zeros((MLA_KV_RANK, nh, LANES - MLA_NOPE_DIM), F32)], axis=2)
    wkv = jnp.concatenate([wk.reshape(MLA_KV_RANK, nh * LANES),
                           wkv3[:, :, MLA_NOPE_DIM:].reshape(MLA_KV_RANK, nh * MLA_V_DIM)],
                          axis=1).astype(BF16)

    inv = ROPE_THETA ** (-jnp.arange(half, dtype=F32) / half)
    ang = jnp.arange(seq).astype(F32)[:, None] * inv[None, :]
    cos, sin = jnp.cos(ang), jnp.sin(ang)
    zl = lambda c: jnp.zeros((seq, c), F32)
    pad = LANES - dq
    c_tab = jnp.concatenate([zl(MLA_NOPE_DIM), cos, cos, zl(pad)], axis=1)
    s_tab = jnp.concatenate([zl(MLA_NOPE_DIM), -sin, sin, zl(pad)], axis=1)
    scale = dq ** -0.5 * LOG2E
    cq_tab = jnp.concatenate([jnp.full((seq, MLA_NOPE_DIM), scale, F32), scale * cos, scale * cos, zl(pad)], axis=1)
    sq_tab = scale * s_tab

    tps = seq // tm
    row = lambda w: pl.BlockSpec((tm, w), lambda i: (i, 0))
    tab = pl.BlockSpec((tm, LANES), lambda i: (i % tps, 0))
    col = lambda w: pl.BlockSpec((w, tm), lambda i: (0, i))
    outs = pl.pallas_call(
        functools.partial(_attn_in_body, tiles_per_seq=tps, scan_steps=int(np.log2(tm))),
        out_shape=[jax.ShapeDtypeStruct((FOX_HEADS * LANES, n), BF16), jax.ShapeDtypeStruct((n, FOX_HEADS * LANES), BF16),
                   jax.ShapeDtypeStruct((wf, n), BF16),
                   jax.ShapeDtypeStruct((nh * LANES, n), BF16), jax.ShapeDtypeStruct((n, nh * LANES), BF16),
                   jax.ShapeDtypeStruct((nh * MLA_V_DIM, n), BF16)],
        grid=(n // tm,),
        in_specs=[row(d), _const_spec((1, d)), _const_spec((d, n1)), _const_spec((1, LANES)),
                  _const_spec((1, MLA_Q_RANK)), _const_spec(wq.shape),
                  _const_spec((1, MLA_KV_RANK)), _const_spec(wkv.shape), tab, tab, tab, tab,
                  _const_spec(place.shape)],
        out_specs=[col(FOX_HEADS * LANES), row(FOX_HEADS * LANES), col(wf),
                   col(nh * LANES), row(nh * LANES), col(nh * MLA_V_DIM)],
        scratch_shapes=[pltpu.VMEM((1, LANES), F32)],
        compiler_params=_params(("arbitrary",)),
        name="attn_in",
    )(x2, g.reshape(1, d), w1, fb, q_norm.reshape(1, -1), wq, kv_norm.reshape(1, -1), wkv,
      c_tab, s_tab, cq_tab, sq_tab, place)
    return outs


def _attn_body(qt_ref, k_ref, vt_ref, o_ref, *, hb, dq, dv, tq, chunk_mask):
    seq = k_ref.shape[1]
    r = lax.broadcasted_iota(jnp.int32, (tq, tq), 0)
    c = lax.broadcasted_iota(jnp.int32, (tq, tq), 1)
    mask = (r // CHUNK <= c // CHUNK) if chunk_mask else (r <= c)

    def scores(i, hh):
        lo, hi = i * tq, (i + 1) * tq
        kl = slice(hh * dq, (hh + 1) * dq)
        qt = qt_ref[kl, lo:hi]
        s_d = jnp.where(mask, _dot(k_ref[0, lo:hi, kl], qt), NEG)
        m = jnp.max(s_d, axis=0, keepdims=True)
        s_m = None
        if i > 0:
            s_m = _dot(k_ref[0, :lo, kl], qt)
            m = jnp.maximum(m, jnp.max(s_m, axis=0, keepdims=True))
        p_d = jnp.exp2(s_d - m)
        l = jnp.sum(p_d, axis=0, keepdims=True)
        p_m = None
        if i > 0:
            p_m = jnp.exp2(s_m - m)
            l = l + jnp.sum(p_m, axis=0, keepdims=True)
            p_m = p_m.astype(BF16)
        return p_d.astype(BF16), p_m, l

    def values(i, hh, p_d, p_m, l):
        lo, hi = i * tq, (i + 1) * tq
        vl = slice(hh * dv, (hh + 1) * dv)
        acc = _dot(vt_ref[vl, lo:hi], p_d)
        if i > 0:
            acc = acc + _dot(vt_ref[vl, :lo], p_m)
        return acc / l

    items = [(i, hh) for i in range(seq // tq) for hh in range(hb)]
    pending, outs = [], []
    for step in range(len(items) + ATTN_PIPELINE_DEPTH):
        if step < len(items):
            pending.append((items[step], scores(*items[step])))
        if step >= ATTN_PIPELINE_DEPTH:
            (i, hh), args = pending.pop(0)
            outs.append(values(i, hh, *args))
            if hh == hb - 1:
                o_ref[0, i * tq:(i + 1) * tq, :] = jnp.concatenate(outs, axis=0).T.astype(o_ref.dtype)
                outs = []


def _attention(qt, k, vt, n_heads, dq, dv, chunk_mask):
    b, seq, _ = k.shape
    hb = 2
    tq = min(ATTN_Q_TILE, seq)
    tok = lambda w: pl.BlockSpec((1, seq, hb * w), lambda bi, hi: (bi, 0, hi))
    feat = lambda w: pl.BlockSpec((hb * w, seq), lambda bi, hi: (hi, bi))
    return pl.pallas_call(
        functools.partial(_attn_body, hb=hb, dq=dq, dv=dv, tq=tq, chunk_mask=chunk_mask),
        out_shape=jax.ShapeDtypeStruct((b, seq, n_heads * dv), BF16),
        grid=(b, n_heads // hb),
        in_specs=[feat(dq), tok(dq), feat(dv)],
        out_specs=tok(dv),
        compiler_params=_params(("parallel", "parallel")),
        name="mla_attn" if chunk_mask else "fox_attn",
    )(qt, k, vt)


def _gdn_in_body(x_ref, g_ref, w_ref, cw_ref, alog_ref, dt_ref,
                 q_ref, k_ref, kt_ref, v_ref, sg_ref, bg_ref, carry_ref,
                 *, tiles_per_seq, scan_steps, cc):
    hd = GDN_HEAD_DIM
    width = GDN_HEADS * hd
    tm = x_ref.shape[0]
    h = _rms_rows(x_ref[...], g_ref[...]).astype(BF16)

    @pl.when(pl.program_id(0) % tiles_per_seq == 0)
    def _():
        carry_ref[:SUBLANES, :] = jnp.zeros((SUBLANES, carry_ref.shape[1]), F32)

    for part, out_ref in enumerate((q_ref, k_ref, v_ref)):
        for c in range(width // cc):
            cols = slice(part * width + c * cc, part * width + (c + 1) * cc)
            y = _dot(h, w_ref[:, cols])
            carry_ref[SUBLANES:, cols] = y
            acc = y * cw_ref[GDN_CONV - 1:GDN_CONV, cols]
            for j in range(1, GDN_CONV):
                shifted = carry_ref[SUBLANES - j:SUBLANES - j + tm, cols]
                acc = acc + shifted * cw_ref[GDN_CONV - 1 - j:GDN_CONV - j, cols]
            carry_ref[:SUBLANES, cols] = y[tm - SUBLANES:]
            a = acc * jax.nn.sigmoid(acc)
            if part < 2:
                heads = []
                for hh in range(cc // hd):
                    xh = a[:, hh * hd:(hh + 1) * hd]
                    xh = xh * lax.rsqrt(jnp.sum(xh * xh, axis=-1, keepdims=True) + 1e-6)
                    heads.append(xh * hd ** -0.5 if part == 0 else xh)
                a = jnp.concatenate(heads, axis=1)
            out_ref[:, c * cc:(c + 1) * cc] = a.astype(BF16)
            if part == 1:
                kt_ref[c * cc:(c + 1) * cc, :] = a.T.astype(BF16)
    for c in range(width // cc):
        gate = _dot(h, w_ref[:, 3 * width + c * cc:3 * width + (c + 1) * cc])
        sg_ref[:, c * cc:(c + 1) * cc] = (gate * jax.nn.sigmoid(gate)).astype(BF16)
    small = _dot(h, w_ref[:, 4 * width:])
    beta = jax.nn.sigmoid(small)
    z = small + dt_ref[...]
    softplus = jnp.maximum(z, 0.0) + jnp.log1p(jnp.exp(-jnp.abs(z)))
    g_log = -jnp.exp(alog_ref[...]) * softplus
    g_cum = _row_scan(g_log, scan_steps, period=CHUNK)
    lanes = lax.broadcasted_iota(jnp.int32, small.shape, 1)
    bg_ref[...] = jnp.where(lanes < GDN_HEADS, beta, g_cum).T[:2 * GDN_HEADS]


def _gdn_in(x2, seq, g, w_in, conv_w, a_log, dt_bias):
    n, d = x2.shape
    tm = min(ROW_TILE, seq)
    nhd = GDN_HEADS
    width = nhd * GDN_HEAD_DIM
    cc = min(GDN_COL_CHUNK, width)
    small = jnp.concatenate([w_in[:, 3 * width:3 * width + 2 * nhd], jnp.zeros((d, LANES - 2 * nhd), F32)], axis=1)
    w1 = jnp.concatenate([w_in[:, :3 * width], w_in[:, 3 * width + 2 * nhd:], small], axis=1).astype(BF16)
    lane_vec = lambda v: jnp.concatenate(
        [jnp.zeros((nhd,), F32), v, jnp.zeros((LANES - 2 * nhd,), F32)]).reshape(1, LANES)
    row = lambda w, dt: (jax.ShapeDtypeStruct((n, w), dt), pl.BlockSpec((tm, w), lambda i: (i, 0)))
    col = lambda w, dt: (jax.ShapeDtypeStruct((w, n), dt), pl.BlockSpec((w, tm), lambda i: (0, i)))
    outs = [row(width, BF16), row(width, BF16), col(width, BF16), row(width, BF16), row(width, BF16),
            col(2 * nhd, F32)]
    return pl.pallas_call(
        functools.partial(_gdn_in_body, tiles_per_seq=seq // tm, scan_steps=int(np.log2(CHUNK)), cc=cc),
        out_shape=[o[0] for o in outs],
        grid=(n // tm,),
        in_specs=[pl.BlockSpec((tm, d), lambda i: (i, 0)), _const_spec((1, d)), _const_spec(w1.shape),
                  _const_spec(conv_w.shape), _const_spec((1, LANES)), _const_spec((1, LANES))],
        out_specs=[o[1] for o in outs],
        scratch_shapes=[pltpu.VMEM((SUBLANES + tm, 3 * width), F32)],
        compiler_params=_params(("arbitrary",)),
        name="gdn_in",
    )(x2, g.reshape(1, d), w1, conv_w, lane_vec(a_log), lane_vec(dt_bias))


def _gdn_body(q_ref, k_ref, kt_ref, v_ref, sg_ref, onorm_ref, g_ref, b_ref, gle_ref, glo_ref,
              o_ref, mq_ref, no_ref, *, hb, pairs_per_step):
    seq = q_ref.shape[1]
    hd = GDN_HEAD_DIM
    pair = 2 * CHUNK
    n_pairs = seq // pair
    ri = lax.broadcasted_iota(jnp.int32, (CHUNK, pair), 0)
    li = lax.broadcasted_iota(jnp.int32, (CHUNK, pair), 1)
    cj = li % CHUNK
    lane_lo = li < CHUNK
    row_lo = lax.broadcasted_iota(jnp.int32, (pair, pair), 0) < CHUNK
    eye = (ri == cj).astype(F32)

    def pack(x):
        return jnp.where(lane_lo, x[:CHUNK], x[CHUNK:])

    def block_diag(x):
        return jnp.concatenate([jnp.where(lane_lo, x, 0.0), jnp.where(lane_lo, 0.0, x)], axis=0).astype(BF16)

    def pair_math(loaded):
        heads = range(len(loaded))
        k2, q2, v2, kt2, g_row, b_row, gl_row = zip(*loaded)
        g_col = [jnp.broadcast_to(g_row[h], (pair, pair)).T for h in heads]
        b_col = [jnp.broadcast_to(b_row[h], (pair, pair)).T for h in heads]
        decay = [jnp.where(ri >= cj, jnp.exp(pack(g_col[h]) - g_row[h]), 0.0) for h in heads]
        gram = [_dot(jnp.concatenate([k2[h][:CHUNK], q2[h][:CHUNK], k2[h][CHUNK:], q2[h][CHUNK:]], axis=0),
                     kt2[h]) for h in heads]
        kk = [jnp.where(lane_lo, gram[h][:CHUNK], gram[h][2 * CHUNK:3 * CHUNK]) for h in heads]
        qk = [jnp.where(lane_lo, gram[h][CHUNK:2 * CHUNK], gram[h][3 * CHUNK:]) for h in heads]
        pw = [jnp.where(ri > cj, -(kk[h] * pack(b_col[h]) * decay[h]), 0.0) for h in heads]
        intra = [qk[h] * decay[h] for h in heads]
        t_mat = [eye + pw[h] for h in heads]
        n_sq = int(np.log2(CHUNK))
        for it in range(n_sq):
            rhs = [block_diag(pw[h]) for h in heads]
            if it == 0:
                pw = [_dot(pw[h].astype(BF16), rhs[h]) for h in heads]
            elif it < n_sq - 1:
                res = [_dot(jnp.concatenate([pw[h], t_mat[h]], axis=0).astype(BF16), rhs[h]) for h in heads]
                pw = [res[h][:CHUNK] for h in heads]
                t_mat = [t_mat[h] + res[h][CHUNK:] for h in heads]
            else:
                t_mat = [t_mat[h] + _dot(t_mat[h].astype(BF16), rhs[h]) for h in heads]
        ub = [_dot(block_diag(t_mat[h] * b_row[h]), v2[h]).astype(BF16) for h in heads]
        wb = [_dot(block_diag(t_mat[h] * (b_row[h] * jnp.exp(g_row[h]))), k2[h]).astype(BF16) for h in heads]
        kt_t = [(kt2[h].astype(F32) * jnp.exp(gl_row[h] - g_row[h])).astype(BF16) for h in heads]
        zero = jnp.zeros_like(wb[0])
        rhs4 = [jnp.concatenate([jnp.where(row_lo, wb[h], zero), jnp.where(row_lo, ub[h], zero),
                                 jnp.where(row_lo, zero, wb[h]), jnp.where(row_lo, zero, ub[h])], axis=1)
                for h in heads]
        mn = [_dot(kt_t[h], rhs4[h]) for h in heads]
        iwu = [_dot(block_diag(intra[h]), jnp.concatenate([wb[h], ub[h]], axis=1)) for h in heads]
        q_eff = [q2[h].astype(F32) * jnp.exp(g_col[h]) - iwu[h][:, :hd] for h in heads]
        return list(zip(mn, iwu, q_eff))

    def pair_step(it, carry):
        items = [(it * pairs_per_step + j, hh) for j in range(pairs_per_step) for hh in range(hb)]
        loaded = []
        for p, hh in items:
            rows = pl.ds(pl.multiple_of(p * pair, pair), pair)
            vec = pl.ds(p, 1)
            hc = slice(hh * hd, (hh + 1) * hd)
            gl_row = jnp.where(lane_lo[:1], gle_ref[0, hh, vec, :], glo_ref[0, hh, vec, :])
            loaded.append((k_ref[0, rows, hc], q_ref[0, rows, hc], v_ref[0, rows, hc], kt_ref[hc, rows],
                           g_ref[0, hh, vec, :], b_ref[0, hh, vec, :], gl_row))
        results = pair_math(loaded)
        for (p, hh), (mn, iwu, q_eff) in zip(items, results):
            for half in range(2):
                ch = 2 * p + half
                rs = slice(half * CHUNK, (half + 1) * CHUNK)
                mq_ref[hh, ch, :hd, :] = mn[:, 2 * half * hd:(2 * half + 1) * hd].astype(BF16)
                mq_ref[hh, ch, hd:, :] = q_eff[rs].astype(BF16)
                no_ref[hh, ch, :hd, :] = mn[:, (2 * half + 1) * hd:(2 * half + 2) * hd]
                no_ref[hh, ch, hd:, :] = iwu[rs, hd:]
        return carry

    lax.fori_loop(0, n_pairs // pairs_per_step, pair_step, 0)

    def scan_step(p, states):
        states = list(states)
        for half, gl_ref in enumerate((gle_ref, glo_ref)):
            ch = 2 * p + half
            rows = pl.ds(pl.multiple_of(ch * CHUNK, CHUNK), CHUNK)
            res = [_dot(mq_ref[hh, ch], states[hh].astype(BF16)) for hh in range(hb)]
            outs = []
            for hh in range(hb):
                outs.append(res[hh][hd:] + no_ref[hh, ch, hd:, :])
                states[hh] = (states[hh] * jnp.exp(gl_ref[0, hh, pl.ds(p, 1), :]) - res[hh][:hd]
                              + no_ref[hh, ch, :hd, :])
            for hh in range(hb):
                hc = slice(hh * hd, (hh + 1) * hd)
                o_ref[0, rows, hc] = (_rms_rows(outs[hh], onorm_ref[...])
                                      * sg_ref[0, rows, hc].astype(F32)).astype(o_ref.dtype)
        return tuple(states)

    lax.fori_loop(0, n_pairs, scan_step, tuple(jnp.zeros((hd, hd), F32) for _ in range(hb)))


def _gdn(q, k, kt, v, sg, out_norm, bg):
    b, seq, _ = q.shape
    nh, hd = GDN_HEADS, GDN_HEAD_DIM
    hb = GDN_HEAD_BLOCK
    pair = 2 * CHUNK
    n_pairs = seq // pair
    bg = bg.reshape(2, nh, b, n_pairs, pair).transpose(0, 2, 1, 3, 4)
    b_rows, g_rows = bg[0], bg[1]
    g_last = g_rows.reshape(b, nh, n_pairs, 2, CHUNK)[..., CHUNK - 1]
    gl_even = jnp.broadcast_to(g_last[..., 0:1], (b, nh, n_pairs, pair))
    gl_odd = jnp.broadcast_to(g_last[..., 1:2], (b, nh, n_pairs, pair))
    tok = pl.BlockSpec((1, seq, hb * hd), lambda bi, hi: (bi, 0, hi))
    vec = pl.BlockSpec((1, hb, n_pairs, pair), lambda bi, hi: (bi, hi, 0, 0))
    return pl.pallas_call(
        functools.partial(_gdn_body, hb=hb, pairs_per_step=GDN_PAIRS_PER_STEP),
        out_shape=jax.ShapeDtypeStruct((b, seq, nh * hd), BF16),
        grid=(b, nh // hb),
        in_specs=[tok, tok, pl.BlockSpec((hb * hd, seq), lambda bi, hi: (hi, bi)), tok, tok,
                  pl.BlockSpec((1, hd), lambda bi, hi: (0, 0)), vec, vec, vec, vec],
        out_specs=tok,
        scratch_shapes=[pltpu.VMEM((hb, seq // CHUNK, hd + CHUNK, hd), BF16),
                        pltpu.VMEM((hb, seq // CHUNK, hd + CHUNK, hd), F32)],
        compiler_params=_params(("parallel", "parallel")),
        name="gdn",
    )(q, k, kt, v, sg, out_norm.reshape(1, hd), g_rows, b_rows, gl_even, gl_odd)


def kernel(x, ffn1_norm, ffn1_w13, ffn1_w2, mix_norm, attn_w_in, fox_f_bias, mla_q_norm, mla_w_uq,
           mla_kv_norm, mla_w_ukv, attn_w_out, gdn_w_in, gdn_conv_w, gdn_a_log, gdn_dt_bias,
           gdn_out_norm, gdn_w_out, ffn2_norm, ffn2_w13, ffn2_w2, final_norm):
    b, seq, d = x.shape
    n = b * seq
    depth = ffn1_norm.shape[0]
    x2 = x.reshape(n, d)
    w13_a, w2_a, w13_b, w2_b = (w.astype(BF16) for w in (ffn1_w13, ffn1_w2, ffn2_w13, ffn2_w2))
    for layer in range(depth):
        x2 = _ffn(x2, layer, ffn1_norm, w13_a, w2_a)
        i = layer // 2
        sh = lambda t: t.reshape(b, seq, -1)
        if layer % 2 == 0:
            qf, kf, vf, qm, km, vm = _attn_in(
                x2, seq, mix_norm[layer], attn_w_in[i], fox_f_bias[i], mla_q_norm[i], mla_w_uq[i],
                mla_kv_norm[i], mla_w_ukv[i])
            o_a = _attention(qf, sh(kf), vf, FOX_HEADS, LANES, FOX_HEAD_DIM, False)
            o_b = _attention(qm, sh(km), vm, MLA_HEADS, LANES, MLA_V_DIM, True)
            wf = FOX_HEADS * FOX_HEAD_DIM
            w_out = attn_w_out[i].astype(BF16)
            mix, mix_w = [o_a.reshape(n, -1), o_b.reshape(n, -1)], [w_out[:wf], w_out[wf:]]
        else:
            q, k, kt, v, sg, bg = _gdn_in(x2, seq, mix_norm[layer], gdn_w_in[i], gdn_conv_w[i],
                                          gdn_a_log[i], gdn_dt_bias[i])
            o = _gdn(sh(q), sh(k), kt, sh(v), sh(sg), gdn_out_norm[i], bg)
            mix, mix_w = [o.reshape(n, -1)], [gdn_w_out[i].astype(BF16)]
        x2 = _ffn(x2, layer, ffn2_norm, w13_b, w2_b, mix, mix_w, final_norm if layer == depth - 1 else None)
    return x2.reshape(b, seq, d)
```

```python
import functools

import numpy as np
import jax
import jax.numpy as jnp
from jax import lax
from jax.experimental import pallas as pl
from jax.experimental.pallas import tpu as pltpu

F32 = jnp.float32
BF16 = jnp.bfloat16

CHUNK = 64
RMS_EPS = 1e-6
FOX_HEADS = 8
FOX_HEAD_DIM = 64
MLA_HEADS = 8
MLA_Q_RANK = 256
MLA_KV_RANK = 128
MLA_NOPE_DIM = 64
MLA_ROPE_DIM = 32
MLA_V_DIM = 64
ROPE_THETA = 10000.0
GDN_HEADS = 8
GDN_HEAD_DIM = 128
GDN_CONV = 4

LANES = 128
SUBLANES = 8
VMEM_LIMIT = 56 * 1024 * 1024
ROW_TILE = 512
ATTN_Q_TILE = 256
ATTN_PIPELINE_DEPTH = 4
GDN_HEAD_BLOCK = 4
GDN_PAIRS_PER_STEP = 4
GDN_COL_CHUNK = 512
NEG = float(jnp.finfo(jnp.float32).min)
LOG2E = float(np.log2(np.e))
N_FORGET_PIECES = 3


def _params(sem):
    return pltpu.CompilerParams(dimension_semantics=sem, vmem_limit_bytes=VMEM_LIMIT)


def _rms_rows(x, g):
    return x * lax.rsqrt(jnp.mean(x * x, axis=-1, keepdims=True) + RMS_EPS) * g


def _dot(a, b):
    return jnp.dot(a, b, preferred_element_type=F32)


def _dot_nt(a, b):
    return lax.dot_general(a, b, (((1,), (1,)), ((), ())), preferred_element_type=F32)


def _const_spec(shape):
    return pl.BlockSpec(shape, lambda *_: (0,) * len(shape), pipeline_mode=pl.Buffered(1))


def _row_scan(y, n_steps, period=None):
    rows = lax.broadcasted_iota(jnp.int32, y.shape, 0)
    if period is not None:
        rows = rows % period
    shift = 1
    for _ in range(n_steps):
        y = y + jnp.where(rows >= shift, pltpu.roll(y, shift, axis=0), 0.0)
        shift *= 2
    return y


def _ffn_body(*refs, d_ff, n_chunks, n_mix, final):
    x_ref, refs = refs[0], refs[1:]
    mix_refs, wo_refs, refs = refs[:n_mix], refs[n_mix:2 * n_mix], refs[2 * n_mix:]
    g_ref, w13_ref, w2_ref = refs[:3]
    fg_ref = refs[3] if final else None
    o_ref, t_ref = refs[-2:]
    x = x_ref[...]
    for m_ref, wo_ref in zip(mix_refs, wo_refs):
        x = x + _dot(m_ref[...], wo_ref[...])
    h = _rms_rows(x, g_ref[...]).astype(BF16)
    fc = d_ff // n_chunks
    for j in range(n_chunks):
        a = _dot(h, w13_ref[:, j * fc:(j + 1) * fc])
        b = _dot(h, w13_ref[:, d_ff + j * fc:d_ff + (j + 1) * fc])
        t_ref[:, j * fc:(j + 1) * fc] = (a * jax.nn.sigmoid(a) * b).astype(BF16)
    out = x + 0.5 * _dot(t_ref[...], w2_ref[...])
    if final:
        out = _rms_rows(out, fg_ref[...])
    o_ref[...] = out


def _ffn(x2, layer, g, w13, w2, mix=(), w_out=(), final_g=None):
    n, d = x2.shape
    d_ff = w2.shape[1]
    tm = min(ROW_TILE, n)
    final = final_g is not None
    row = lambda w: pl.BlockSpec((tm, w), lambda i: (i, 0))
    layer_spec = lambda *shape: pl.BlockSpec((None,) + shape, lambda i: (layer,) + (0,) * len(shape),
                                             pipeline_mode=pl.Buffered(1))
    args = [x2, *mix, *w_out, g.reshape(-1, 1, d), w13, w2]
    specs = ([row(d)] + [row(m.shape[1]) for m in mix] + [_const_spec(w.shape) for w in w_out]
             + [layer_spec(1, d), layer_spec(d, 2 * d_ff), layer_spec(d_ff, d)])
    if final:
        args.append(final_g.reshape(1, d))
        specs.append(_const_spec((1, d)))
    return pl.pallas_call(
        functools.partial(_ffn_body, d_ff=d_ff, n_chunks=d_ff // 256, n_mix=len(mix), final=final),
        out_shape=jax.ShapeDtypeStruct((n, d), F32),
        grid=(n // tm,),
        in_specs=specs,
        out_specs=row(d),
        scratch_shapes=[pltpu.VMEM((tm, d_ff), BF16)],
        compiler_params=_params(("parallel",)),
        name="ffn",
    )(*args)


def _attn_in_body(x_ref, g_ref, w_ref, fb_ref, qn_ref, wq_ref, kvn_ref, wkv_ref,
                  ck_ref, sk_ref, cq_ref, sq_ref, place_ref,
                  qf_ref, kf_ref, vf_ref, qm_ref, km_ref, vm_ref, carry_ref,
                  *, tiles_per_seq, scan_steps):
    wf = FOX_HEADS * FOX_HEAD_DIM
    hdim = FOX_HEAD_DIM
    tm = x_ref.shape[0]
    h = _rms_rows(x_ref[...], g_ref[...]).astype(BF16)
    y = _dot(h, w_ref[...])
    q_t = (y[:, :wf] * (hdim ** -0.5 * LOG2E)).T
    spare = jnp.where(lax.broadcasted_iota(jnp.int32, (LANES - hdim, tm), 0) < N_FORGET_PIECES, -1.0, 0.0)
    qf_ref[...] = jnp.concatenate(
        [blk for hh in range(FOX_HEADS) for blk in (q_t[hh * hdim:(hh + 1) * hdim], spare)], axis=0).astype(BF16)
    vf_ref[...] = y[:, 2 * wf:3 * wf].T.astype(BF16)
    o = 3 * wf
    c_q = y[:, o:o + MLA_Q_RANK]
    o += MLA_Q_RANK
    c_kv = y[:, o:o + MLA_KV_RANK]
    o += MLA_KV_RANK
    kp = y[:, o:o + LANES]
    kq = y[:, o + LANES:o + 2 * LANES]
    f_logit = y[:, o + 2 * LANES:o + 3 * LANES]

    z = f_logit + fb_ref[...]
    log_f = jnp.minimum(z, 0.0) - jnp.log1p(jnp.exp(-jnp.abs(z)))

    @pl.when(pl.program_id(0) % tiles_per_seq == 0)
    def _():
        carry_ref[...] = jnp.zeros_like(carry_ref)

    f_cum = _row_scan(log_f, scan_steps) + carry_ref[...]
    carry_ref[...] = f_cum[-1:, :]
    f_hi = (f_cum * LOG2E).astype(BF16)
    r1 = f_cum * LOG2E - f_hi.astype(F32)
    f_mid = r1.astype(BF16)
    f_lo = (r1 - f_mid.astype(F32)).astype(BF16)
    placed = _dot(jnp.concatenate([f_hi, f_mid, f_lo], axis=1), place_ref[...])
    low = lax.broadcasted_iota(jnp.int32, (tm, LANES), 1) < hdim
    for j in range(wf // LANES):
        kk = y[:, wf + j * LANES:wf + (j + 1) * LANES]
        for hh, src in enumerate((kk, pltpu.roll(kk, hdim, axis=1))):
            cols = slice((2 * j + hh) * LANES, (2 * j + hh + 1) * LANES)
            kf_ref[:, cols] = jnp.where(low, src, placed[:, cols]).astype(BF16)

    nh = MLA_HEADS
    qb = _dot(_rms_rows(c_q, qn_ref[...]).astype(BF16), wq_ref[...])
    cq = jnp.concatenate([cq_ref[...]] * nh, axis=1)
    sq = jnp.concatenate([sq_ref[...]] * nh, axis=1)
    qm_ref[...] = (qb[:, :nh * LANES] * cq + qb[:, nh * LANES:] * sq).T.astype(BF16)
    kvb = _dot(_rms_rows(c_kv, kvn_ref[...]).astype(BF16), wkv_ref[...])
    k_rot = kp * ck_ref[...] + kq * sk_ref[...]
    km_ref[...] = (kvb[:, :nh * LANES] + jnp.concatenate([k_rot] * nh, axis=1)).astype(BF16)
    vm_ref[...] = kvb[:, nh * LANES:].T.astype(BF16)


def _attn_in(x2, seq, g, w_in, f_bias, q_norm, w_uq, kv_norm, w_ukv):
    n, d = x2.shape
    tm = min(ROW_TILE, seq)
    wf = FOX_HEADS * FOX_HEAD_DIM
    half = MLA_ROPE_DIM // 2
    nh = MLA_HEADS

    qa, ka, va, fl, cq, ckv, kpe = jnp.split(
        w_in, np.cumsum([wf, wf, wf, FOX_HEADS, MLA_Q_RANK, MLA_KV_RANK]).tolist(), axis=1)
    z = lambda c: jnp.zeros((d, c), F32)
    p_blk = jnp.concatenate([z(MLA_NOPE_DIM), kpe, z(LANES - MLA_NOPE_DIM - MLA_ROPE_DIM)], axis=1)
    q_blk = jnp.concatenate([z(MLA_NOPE_DIM), kpe[:, half:], kpe[:, :half],
                             z(LANES - MLA_NOPE_DIM - MLA_ROPE_DIM)], axis=1)
    f_blk = jnp.concatenate([fl, z(LANES - FOX_HEADS)], axis=1)
    w1 = jnp.concatenate([qa, ka, va, cq, ckv, p_blk, q_blk, f_blk], axis=1).astype(BF16)
    n1 = w1.shape[1]
    fb = jnp.concatenate([f_bias, jnp.zeros((LANES - FOX_HEADS,), F32)]).reshape(1, LANES)
    n_piece = N_FORGET_PIECES
    src = (np.arange(n_piece)[:, None] * LANES + np.arange(FOX_HEADS)[None, :]).ravel()
    dst = (np.arange(FOX_HEADS)[None, :] * LANES + FOX_HEAD_DIM + np.arange(n_piece)[:, None]).ravel()
    place_np = np.zeros((n_piece * LANES, FOX_HEADS * LANES), np.float32)
    place_np[src, dst] = 1.0
    place = jnp.asarray(place_np, BF16)

    dq = MLA_NOPE_DIM + MLA_ROPE_DIM
    wq3 = w_uq.reshape(MLA_Q_RANK, nh, dq)
    zq = jnp.zeros((MLA_Q_RANK, nh, LANES - dq), F32)
    wa = jnp.concatenate([wq3, zq], axis=2).reshape(MLA_Q_RANK, nh * LANES)
    wb = jnp.concatenate([jnp.zeros((MLA_Q_RANK, nh, MLA_NOPE_DIM), F32),
                          wq3[:, :, MLA_NOPE_DIM + half:], wq3[:, :, MLA_NOPE_DIM:MLA_NOPE_DIM + half],
                          zq], axis=2).reshape(MLA_Q_RANK, nh * LANES)
    wq = jnp.concatenate([wa, wb], axis=1).astype(BF16)
    wkv3 = w_ukv.reshape(MLA_KV_RANK, nh, MLA_NOPE_DIM + MLA_V_DIM)
    wk = jnp.concatenate([wkv3[:, :, :MLA_NOPE_DIM],
                          jnp.zeros((MLA_KV_RANK, nh, LANES - MLA_NOPE_DIM), F32)], axis=2)
    wkv = jnp.concatenate([wk.reshape(MLA_KV_RANK, nh * LANES),
                           wkv3[:, :, MLA_NOPE_DIM:].reshape(MLA_KV_RANK, nh * MLA_V_DIM)],
                          axis=1).astype(BF16)

    inv = ROPE_THETA ** (-jnp.arange(half, dtype=F32) / half)
    ang = jnp.arange(seq).astype(F32)[:, None] * inv[None, :]
    cos, sin = jnp.cos(ang), jnp.sin(ang)
    zl = lambda c: jnp.zeros((seq, c), F32)
    pad = LANES - dq
    c_tab = jnp.concatenate([zl(MLA_NOPE_DIM), cos, cos, zl(pad)], axis=1)
    s_tab = jnp.concatenate([zl(MLA_NOPE_DIM), -sin, sin, zl(pad)], axis=1)
    scale = dq ** -0.5 * LOG2E
    cq_tab = jnp.concatenate([jnp.full((seq, MLA_NOPE_DIM), scale, F32), scale * cos, scale * cos, zl(pad)], axis=1)
    sq_tab = scale * s_tab

    tps = seq // tm
    row = lambda w: pl.BlockSpec((tm, w), lambda i: (i, 0))
    tab = pl.BlockSpec((tm, LANES), lambda i: (i % tps, 0))
    col = lambda w: pl.BlockSpec((w, tm), lambda i: (0, i))
    outs = pl.pallas_call(
        functools.partial(_attn_in_body, tiles_per_seq=tps, scan_steps=int(np.log2(tm))),
        out_shape=[jax.ShapeDtypeStruct((FOX_HEADS * LANES, n), BF16), jax.ShapeDtypeStruct((n, FOX_HEADS * LANES), BF16),
                   jax.ShapeDtypeStruct((wf, n), BF16),
                   jax.ShapeDtypeStruct((nh * LANES, n), BF16), jax.ShapeDtypeStruct((n, nh * LANES), BF16),
                   jax.ShapeDtypeStruct((nh * MLA_V_DIM, n), BF16)],
        grid=(n // tm,),
        in_specs=[row(d), _const_spec((1, d)), _const_spec((d, n1)), _const_spec((1, LANES)),
                  _const_spec((1, MLA_Q_RANK)), _const_spec(wq.shape),
                  _const_spec((1, MLA_KV_RANK)), _const_spec(wkv.shape), tab, tab, tab, tab,
                  _const_spec(place.shape)],
        out_specs=[col(FOX_HEADS * LANES), row(FOX_HEADS * LANES), col(wf),
                   col(nh * LANES), row(nh * LANES), col(nh * MLA_V_DIM)],
        scratch_shapes=[pltpu.VMEM((1, LANES), F32)],
        compiler_params=_params(("arbitrary",)),
        name="attn_in",
    )(x2, g.reshape(1, d), w1, fb, q_norm.reshape(1, -1), wq, kv_norm.reshape(1, -1), wkv,
      c_tab, s_tab, cq_tab, sq_tab, place)
    return outs


def _attn_body(qt_ref, k_ref, vt_ref, o_ref, *, hb, dq, dv, tq, chunk_mask):
    seq = k_ref.shape[1]
    r = lax.broadcasted_iota(jnp.int32, (tq, tq), 0)
    c = lax.broadcasted_iota(jnp.int32, (tq, tq), 1)
    mask = (r // CHUNK <= c // CHUNK) if chunk_mask else (r <= c)

    def scores(i, hh):
        lo, hi = i * tq, (i + 1) * tq
        kl = slice(hh * dq, (hh + 1) * dq)
        qt = qt_ref[kl, lo:hi]
        s_d = jnp.where(mask, _dot(k_ref[0, lo:hi, kl], qt), NEG)
        m = jnp.max(s_d, axis=0, keepdims=True)
        s_m = None
        if i > 0:
            s_m = _dot(k_ref[0, :lo, kl], qt)
            m = jnp.maximum(m, jnp.max(s_m, axis=0, keepdims=True))
        p_d = jnp.exp2(s_d - m)
        l = jnp.sum(p_d, axis=0, keepdims=True)
        p_m = None
        if i > 0:
            p_m = jnp.exp2(s_m - m)
            l = l + jnp.sum(p_m, axis=0, keepdims=True)
            p_m = p_m.astype(BF16)
        return p_d.astype(BF16), p_m, l

    def values(i, hh, p_d, p_m, l):
        lo, hi = i * tq, (i + 1) * tq
        vl = slice(hh * dv, (hh + 1) * dv)
        acc = _dot(vt_ref[vl, lo:hi], p_d)
        if i > 0:
            acc = acc + _dot(vt_ref[vl, :lo], p_m)
        return acc / l

    items = [(i, hh) for i in range(seq // tq) for hh in range(hb)]
    pending, outs = [], []
    for step in range(len(items) + ATTN_PIPELINE_DEPTH):
        if step < len(items):
            pending.append((items[step], scores(*items[step])))
        if step >= ATTN_PIPELINE_DEPTH:
            (i, hh), args = pending.pop(0)
            outs.append(values(i, hh, *args))
            if hh == hb - 1:
                o_ref[0, i * tq:(i + 1) * tq, :] = jnp.concatenate(outs, axis=0).T.astype(o_ref.dtype)
                outs = []


def _attention(qt, k, vt, n_heads, dq, dv, chunk_mask):
    b, seq, _ = k.shape
    hb = 2
    tq = min(ATTN_Q_TILE, seq)
    tok = lambda w: pl.BlockSpec((1, seq, hb * w), lambda bi, hi: (bi, 0, hi))
    feat = lambda w: pl.BlockSpec((hb * w, seq), lambda bi, hi: (hi, bi))
    return pl.pallas_call(
        functools.partial(_attn_body, hb=hb, dq=dq, dv=dv, tq=tq, chunk_mask=chunk_mask),
        out_shape=jax.ShapeDtypeStruct((b, seq, n_heads * dv), BF16),
        grid=(b, n_heads // hb),
        in_specs=[feat(dq), tok(dq), feat(dv)],
        out_specs=tok(dv),
        compiler_params=_params(("parallel", "parallel")),
        name="mla_attn" if chunk_mask else "fox_attn",
    )(qt, k, vt)


def _gdn_in_body(x_ref, g_ref, w_ref, cw_ref, alog_ref, dt_ref,
                 q_ref, k_ref, kt_ref, v_ref, sg_ref, bg_ref, carry_ref,
                 *, tiles_per_seq, scan_steps, cc):
    hd = GDN_HEAD_DIM
    width = GDN_HEADS * hd
    tm = x_ref.shape[0]
    h = _rms_rows(x_ref[...], g_ref[...]).astype(BF16)

    @pl.when(pl.program_id(0) % tiles_per_seq == 0)
    def _():
        carry_ref[...] = jnp.zeros_like(carry_ref)

    nv = tm // SUBLANES
    sub = lax.broadcasted_iota(jnp.int32, (nv, SUBLANES, cc), 1)
    for part, out_ref in enumerate((q_ref, k_ref, v_ref)):
        for c in range(width // cc):
            cols = slice(part * width + c * cc, part * width + (c + 1) * cc)
            y = _dot(h, w_ref[:, cols])
            prev = carry_ref[:, cols]
            carry_ref[:, cols] = y[tm - SUBLANES:]
            acc = y * cw_ref[GDN_CONV - 1:GDN_CONV, cols]
            y3 = y.reshape(nv, SUBLANES, cc)
            for j in range(1, GDN_CONV):
                rot = pltpu.roll(y3, j, axis=1)
                rot_prev = jnp.concatenate([pltpu.roll(prev, j, axis=0)[None], rot[:-1]], axis=0)
                shifted = jnp.where(sub < j, rot_prev, rot).reshape(tm, cc)
                acc = acc + shifted * cw_ref[GDN_CONV - 1 - j:GDN_CONV - j, cols]
            a = acc * jax.nn.sigmoid(acc)
            if part < 2:
                heads = []
                for hh in range(cc // hd):
                    xh = a[:, hh * hd:(hh + 1) * hd]
                    xh = xh * lax.rsqrt(jnp.sum(xh * xh, axis=-1, keepdims=True) + 1e-6)
                    heads.append(xh * hd ** -0.5 if part == 0 else xh)
                a = jnp.concatenate(heads, axis=1)
            out_ref[:, c * cc:(c + 1) * cc] = a.astype(BF16)
            if part == 1:
                kt_ref[c * cc:(c + 1) * cc, :] = a.T.astype(BF16)
    for c in range(width // cc):
        gate = _dot(h, w_ref[:, 3 * width + c * cc:3 * width + (c + 1) * cc])
        sg_ref[:, c * cc:(c + 1) * cc] = (gate * jax.nn.sigmoid(gate)).astype(BF16)
    small = _dot(h, w_ref[:, 4 * width:])
    beta = jax.nn.sigmoid(small)
    z = small + dt_ref[...]
    softplus = jnp.maximum(z, 0.0) + jnp.log1p(jnp.exp(-jnp.abs(z)))
    g_log = -jnp.exp(alog_ref[...]) * softplus
    g_cum = _row_scan(g_log, scan_steps, period=CHUNK)
    lanes = lax.broadcasted_iota(jnp.int32, small.shape, 1)
    bg_ref[...] = jnp.where(lanes < GDN_HEADS, beta, g_cum).T[:2 * GDN_HEADS]


def _gdn_in(x2, seq, g, w_in, conv_w, a_log, dt_bias):
    n, d = x2.shape
    tm = min(ROW_TILE, seq)
    nhd = GDN_HEADS
    width = nhd * GDN_HEAD_DIM
    cc = min(GDN_COL_CHUNK, width)
    small = jnp.concatenate([w_in[:, 3 * width:3 * width + 2 * nhd], jnp.zeros((d, LANES - 2 * nhd), F32)], axis=1)
    w1 = jnp.concatenate([w_in[:, :3 * width], w_in[:, 3 * width + 2 * nhd:], small], axis=1).astype(BF16)
    lane_vec = lambda v: jnp.concatenate(
        [jnp.zeros((nhd,), F32), v, jnp.zeros((LANES - 2 * nhd,), F32)]).reshape(1, LANES)
    row = lambda w, dt: (jax.ShapeDtypeStruct((n, w), dt), pl.BlockSpec((tm, w), lambda i: (i, 0)))
    col = lambda w, dt: (jax.ShapeDtypeStruct((w, n), dt), pl.BlockSpec((w, tm), lambda i: (0, i)))
    outs = [row(width, BF16), row(width, BF16), col(width, BF16), row(width, BF16), row(width, BF16),
            col(2 * nhd, F32)]
    return pl.pallas_call(
        functools.partial(_gdn_in_body, tiles_per_seq=seq // tm, scan_steps=int(np.log2(CHUNK)), cc=cc),
        out_shape=[o[0] for o in outs],
        grid=(n // tm,),
        in_specs=[pl.BlockSpec((tm, d), lambda i: (i, 0)), _const_spec((1, d)), _const_spec(w1.shape),
                  _const_spec(conv_w.shape), _const_spec((1, LANES)), _const_spec((1, LANES))],
        out_specs=[o[1] for o in outs],
        scratch_shapes=[pltpu.VMEM((SUBLANES, 3 * width), F32)],
        compiler_params=_params(("arbitrary",)),
        name="gdn_in",
    )(x2, g.reshape(1, d), w1, conv_w, lane_vec(a_log), lane_vec(dt_bias))


def _gdn_body(q_ref, k_ref, kt_ref, v_ref, sg_ref, onorm_ref, g_ref, b_ref, gle_ref, glo_ref,
              o_ref, mq_ref, no_ref, *, hb, pairs_per_step):
    seq = q_ref.shape[1]
    hd = GDN_HEAD_DIM
    pair = 2 * CHUNK
    n_pairs = seq // pair
    ri = lax.broadcasted_iota(jnp.int32, (CHUNK, pair), 0)
    li = lax.broadcasted_iota(jnp.int32, (CHUNK, pair), 1)
    cj = li % CHUNK
    lane_lo = li < CHUNK
    row_lo = lax.broadcasted_iota(jnp.int32, (pair, pair), 0) < CHUNK
    eye = (ri == cj).astype(F32)

    def pack(x):
        return jnp.where(lane_lo, x[:CHUNK], x[CHUNK:])

    def block_diag(x):
        return jnp.concatenate([jnp.where(lane_lo, x, 0.0), jnp.where(lane_lo, 0.0, x)], axis=0).astype(BF16)

    def pair_math(loaded):
        heads = range(len(loaded))
        k2, q2, v2, kt2, g_row, b_row, gl_row = zip(*loaded)
        g_col = [jnp.broadcast_to(g_row[h], (pair, pair)).T for h in heads]
        b_col = [jnp.broadcast_to(b_row[h], (pair, pair)).T for h in heads]
        decay = [jnp.where(ri >= cj, jnp.exp(pack(g_col[h]) - g_row[h]), 0.0) for h in heads]
        gram = [_dot(jnp.concatenate([k2[h][:CHUNK], q2[h][:CHUNK], k2[h][CHUNK:], q2[h][CHUNK:]], axis=0),
                     kt2[h]) for h in heads]
        kk = [jnp.where(lane_lo, gram[h][:CHUNK], gram[h][2 * CHUNK:3 * CHUNK]) for h in heads]
        qk = [jnp.where(lane_lo, gram[h][CHUNK:2 * CHUNK], gram[h][3 * CHUNK:]) for h in heads]
        pw = [jnp.where(ri > cj, -(kk[h] * pack(b_col[h]) * decay[h]), 0.0) for h in heads]
        intra = [qk[h] * decay[h] for h in heads]
        t_mat = [eye + pw[h] for h in heads]
        n_sq = int(np.log2(CHUNK))
        for it in range(n_sq):
            rhs = [block_diag(pw[h]) for h in heads]
            if it == 0:
                pw = [_dot(pw[h].astype(BF16), rhs[h]) for h in heads]
            elif it < n_sq - 1:
                res = [_dot(jnp.concatenate([pw[h], t_mat[h]], axis=0).astype(BF16), rhs[h]) for h in heads]
                pw = [res[h][:CHUNK] for h in heads]
                t_mat = [t_mat[h] + res[h][CHUNK:] for h in heads]
            else:
                t_mat = [t_mat[h] + _dot(t_mat[h].astype(BF16), rhs[h]) for h in heads]
        ub = [_dot(block_diag(t_mat[h] * b_row[h]), v2[h]).astype(BF16) for h in heads]
        wb = [_dot(block_diag(t_mat[h] * (b_row[h] * jnp.exp(g_row[h]))), k2[h]).astype(BF16) for h in heads]
        kt_t = [(kt2[h].astype(F32) * jnp.exp(gl_row[h] - g_row[h])).astype(BF16) for h in heads]
        zero = jnp.zeros_like(wb[0])
        rhs4 = [jnp.concatenate([jnp.where(row_lo, wb[h], zero), jnp.where(row_lo, ub[h], zero),
                                 jnp.where(row_lo, zero, wb[h]), jnp.where(row_lo, zero, ub[h])], axis=1)
                for h in heads]
        mn = [_dot(kt_t[h], rhs4[h]) for h in heads]
        iwu = [_dot(block_diag(intra[h]), jnp.concatenate([wb[h], ub[h]], axis=1)) for h in heads]
        q_eff = [q2[h].astype(F32) * jnp.exp(g_col[h]) - iwu[h][:, :hd] for h in heads]
        return list(zip(mn, iwu, q_eff))

    def pair_step(it, carry):
        items = [(it * pairs_per_step + j, hh) for j in range(pairs_per_step) for hh in range(hb)]
        loaded = []
        for p, hh in items:
            rows = pl.ds(pl.multiple_of(p * pair, pair), pair)
            vec = pl.ds(p, 1)
            hc = slice(hh * hd, (hh + 1) * hd)
            gl_row = jnp.where(lane_lo[:1], gle_ref[0, hh, vec, :], glo_ref[0, hh, vec, :])
            loaded.append((k_ref[0, rows, hc], q_ref[0, rows, hc], v_ref[0, rows, hc], kt_ref[hc, rows],
                           g_ref[0, hh, vec, :], b_ref[0, hh, vec, :], gl_row))
        results = pair_math(loaded)
        for (p, hh), (mn, iwu, q_eff) in zip(items, results):
            for half in range(2):
                ch = 2 * p + half
                rs = slice(half * CHUNK, (half + 1) * CHUNK)
                mq_ref[hh, ch, :hd, :] = mn[:, 2 * half * hd:(2 * half + 1) * hd].astype(BF16)
                mq_ref[hh, ch, hd:, :] = q_eff[rs].astype(BF16)
                no_ref[hh, ch, :hd, :] = mn[:, (2 * half + 1) * hd:(2 * half + 2) * hd]
                no_ref[hh, ch, hd:, :] = iwu[rs, hd:]
        return carry

    lax.fori_loop(0, n_pairs // pairs_per_step, pair_step, 0)

    def scan_step(p, states):
        states = list(states)
        for half, gl_ref in enumerate((gle_ref, glo_ref)):
            ch = 2 * p + half
            rows = pl.ds(pl.multiple_of(ch * CHUNK, CHUNK), CHUNK)
            res = [_dot(mq_ref[hh, ch], states[hh].astype(BF16)) for hh in range(hb)]
            outs = []
            for hh in range(hb):
                outs.append(res[hh][hd:] + no_ref[hh, ch, hd:, :])
                states[hh] = (states[hh] * jnp.exp(gl_ref[0, hh, pl.ds(p, 1), :]) - res[hh][:hd]
                              + no_ref[hh, ch, :hd, :])
            for hh in range(hb):
                hc = slice(hh * hd, (hh + 1) * hd)
                o_ref[0, rows, hc] = (_rms_rows(outs[hh], onorm_ref[...])
                                      * sg_ref[0, rows, hc].astype(F32)).astype(o_ref.dtype)
        return tuple(states)

    lax.fori_loop(0, n_pairs, scan_step, tuple(jnp.zeros((hd, hd), F32) for _ in range(hb)))


def _gdn(q, k, kt, v, sg, out_norm, bg):
    b, seq, _ = q.shape
    nh, hd = GDN_HEADS, GDN_HEAD_DIM
    hb = GDN_HEAD_BLOCK
    pair = 2 * CHUNK
    n_pairs = seq // pair
    bg = bg.reshape(2, nh, b, n_pairs, pair).transpose(0, 2, 1, 3, 4)
    b_rows, g_rows = bg[0], bg[1]
    g_last = g_rows.reshape(b, nh, n_pairs, 2, CHUNK)[..., CHUNK - 1]
    gl_even = jnp.broadcast_to(g_last[..., 0:1], (b, nh, n_pairs, pair))
    gl_odd = jnp.broadcast_to(g_last[..., 1:2], (b, nh, n_pairs, pair))
    tok = pl.BlockSpec((1, seq, hb * hd), lambda bi, hi: (bi, 0, hi))
    vec = pl.BlockSpec((1, hb, n_pairs, pair), lambda bi, hi: (bi, hi, 0, 0))
    return pl.pallas_call(
        functools.partial(_gdn_body, hb=hb, pairs_per_step=GDN_PAIRS_PER_STEP),
        out_shape=jax.ShapeDtypeStruct((b, seq, nh * hd), BF16),
        grid=(b, nh // hb),
        in_specs=[tok, tok, pl.BlockSpec((hb * hd, seq), lambda bi, hi: (hi, bi)), tok, tok,
                  pl.BlockSpec((1, hd), lambda bi, hi: (0, 0)), vec, vec, vec, vec],
        out_specs=tok,
        scratch_shapes=[pltpu.VMEM((hb, seq // CHUNK, hd + CHUNK, hd), BF16),
                        pltpu.VMEM((hb, seq // CHUNK, hd + CHUNK, hd), F32)],
        compiler_params=_params(("parallel", "parallel")),
        name="gdn",
    )(q, k, kt, v, sg, out_norm.reshape(1, hd), g_rows, b_rows, gl_even, gl_odd)


def kernel(x, ffn1_norm, ffn1_w13, ffn1_w2, mix_norm, attn_w_in, fox_f_bias, mla_q_norm, mla_w_uq,
           mla_kv_norm, mla_w_ukv, attn_w_out, gdn_w_in, gdn_conv_w, gdn_a_log, gdn_dt_bias,
           gdn_out_norm, gdn_w_out, ffn2_norm, ffn2_w13, ffn2_w2, final_norm):
    b, seq, d = x.shape
    n = b * seq
    depth = ffn1_norm.shape[0]
    x2 = x.reshape(n, d)
    w13_a, w2_a, w13_b, w2_b = (w.astype(BF16) for w in (ffn1_w13, ffn1_w2, ffn2_w13, ffn2_w2))
    for layer in range(depth):
        x2 = _ffn(x2, layer, ffn1_norm, w13_a, w2_a)
        i = layer // 2
        sh = lambda t: t.reshape(b, seq, -1)
        if layer % 2 == 0:
            qf, kf, vf, qm, km, vm = _attn_in(
                x2, seq, mix_norm[layer], attn_w_in[i], fox_f_bias[i], mla_q_norm[i], mla_w_uq[i],
                mla_kv_norm[i], mla_w_ukv[i])
            o_a = _attention(qf, sh(kf), vf, FOX_HEADS, LANES, FOX_HEAD_DIM, False)
            o_b = _attention(qm, sh(km), vm, MLA_HEADS, LANES, MLA_V_DIM, True)
            wf = FOX_HEADS * FOX_HEAD_DIM
            w_out = attn_w_out[i].astype(BF16)
            mix, mix_w = [o_a.reshape(n, -1), o_b.reshape(n, -1)], [w_out[:wf], w_out[wf:]]
        else:
            q, k, kt, v, sg, bg = _gdn_in(x2, seq, mix_norm[layer], gdn_w_in[i], gdn_conv_w[i],
                                          gdn_a_log[i], gdn_dt_bias[i])
            o = _gdn(sh(q), sh(k), kt, sh(v), sh(sg), gdn_out_norm[i], bg)
            mix, mix_w = [o.reshape(n, -1)], [gdn_w_out[i].astype(BF16)]
        x2 = _ffn(x2, layer, ffn2_norm, w13_b, w2_b, mix, mix_w, final_norm if layer == depth - 1 else None)
    return x2.reshape(b, seq, d)
```

```python
import functools

import numpy as np
import jax
import jax.numpy as jnp
from jax import lax
from jax.experimental import pallas as pl
from jax.experimental.pallas import tpu as pltpu

F32 = jnp.float32
BF16 = jnp.bfloat16

CHUNK = 64
RMS_EPS = 1e-6
FOX_HEADS = 8
FOX_HEAD_DIM = 64
MLA_HEADS = 8
MLA_Q_RANK = 256
MLA_KV_RANK = 128
MLA_NOPE_DIM = 64
MLA_ROPE_DIM = 32
MLA_V_DIM = 64
ROPE_THETA = 10000.0
GDN_HEADS = 8
GDN_HEAD_DIM = 128
GDN_CONV = 4

LANES = 128
SUBLANES = 8
VMEM_LIMIT = 56 * 1024 * 1024
ROW_TILE = 512
ATTN_Q_TILE = 256
ATTN_PIPELINE_DEPTH = 4
GDN_HEAD_BLOCK = 4
GDN_PAIRS_PER_STEP = 4
GDN_COL_CHUNK = 512
NEG = float(jnp.finfo(jnp.float32).min)
LOG2E = float(np.log2(np.e))
ONES_ROWS = 16
N_FORGET_PIECES = 3


def _params(sem):
    return pltpu.CompilerParams(dimension_semantics=sem, vmem_limit_bytes=VMEM_LIMIT)


def _rms_rows(x, g):
    return x * lax.rsqrt(jnp.mean(x * x, axis=-1, keepdims=True) + RMS_EPS) * g


def _dot(a, b):
    return jnp.dot(a, b, preferred_element_type=F32)


def _dot_nt(a, b):
    return lax.dot_general(a, b, (((1,), (1,)), ((), ())), preferred_element_type=F32)


def _const_spec(shape):
    return pl.BlockSpec(shape, lambda *_: (0,) * len(shape), pipeline_mode=pl.Buffered(1))


def _row_scan(y, n_steps, period=None):
    rows = lax.broadcasted_iota(jnp.int32, y.shape, 0)
    if period is not None:
        rows = rows % period
    shift = 1
    for _ in range(n_steps):
        y = y + jnp.where(rows >= shift, pltpu.roll(y, shift, axis=0), 0.0)
        shift *= 2
    return y


def _ffn_body(*refs, d_ff, n_chunks, n_mix, final):
    x_ref, refs = refs[0], refs[1:]
    mix_refs, wo_refs, refs = refs[:n_mix], refs[n_mix:2 * n_mix], refs[2 * n_mix:]
    g_ref, w13_ref, w2_ref = refs[:3]
    fg_ref = refs[3] if final else None
    o_ref, t_ref = refs[-2:]
    x = x_ref[...]
    for m_ref, wo_ref in zip(mix_refs, wo_refs):
        x = x + _dot(m_ref[...], wo_ref[...])
    h = _rms_rows(x, g_ref[...]).astype(BF16)
    fc = d_ff // n_chunks
    for j in range(n_chunks):
        a = _dot(h, w13_ref[:, j * fc:(j + 1) * fc])
        b = _dot(h, w13_ref[:, d_ff + j * fc:d_ff + (j + 1) * fc])
        t_ref[:, j * fc:(j + 1) * fc] = (a * jax.nn.sigmoid(a) * b).astype(BF16)
    out = x + 0.5 * _dot(t_ref[...], w2_ref[...])
    if final:
        out = _rms_rows(out, fg_ref[...])
    o_ref[...] = out


def _ffn(x2, layer, g, w13, w2, mix=(), w_out=(), final_g=None):
    n, d = x2.shape
    d_ff = w2.shape[1]
    tm = min(ROW_TILE, n)
    final = final_g is not None
    row = lambda w: pl.BlockSpec((tm, w), lambda i: (i, 0))
    layer_spec = lambda *shape: pl.BlockSpec((None,) + shape, lambda i: (layer,) + (0,) * len(shape),
                                             pipeline_mode=pl.Buffered(1))
    args = [x2, *mix, *w_out, g.reshape(-1, 1, d), w13, w2]
    specs = ([row(d)] + [row(m.shape[1]) for m in mix] + [_const_spec(w.shape) for w in w_out]
             + [layer_spec(1, d), layer_spec(d, 2 * d_ff), layer_spec(d_ff, d)])
    if final:
        args.append(final_g.reshape(1, d))
        specs.append(_const_spec((1, d)))
    return pl.pallas_call(
        functools.partial(_ffn_body, d_ff=d_ff, n_chunks=d_ff // 256, n_mix=len(mix), final=final),
        out_shape=jax.ShapeDtypeStruct((n, d), F32),
        grid=(n // tm,),
        in_specs=specs,
        out_specs=row(d),
        scratch_shapes=[pltpu.VMEM((tm, d_ff), BF16)],
        compiler_params=_params(("parallel",)),
        name="ffn",
    )(*args)


def _attn_in_body(x_ref, g_ref, w_ref, fb_ref, qn_ref, wq_ref, kvn_ref, wkv_ref,
                  ck_ref, sk_ref, cq_ref, sq_ref, place_ref,
                  qf_ref, kf_ref, vf_ref, qm_ref, km_ref, vm_ref, carry_ref,
                  *, tiles_per_seq, scan_steps):
    wf = FOX_HEADS * FOX_HEAD_DIM
    hdim = FOX_HEAD_DIM
    tm = x_ref.shape[0]
    h = _rms_rows(x_ref[...], g_ref[...]).astype(BF16)
    y = _dot(h, w_ref[...])
    q_t = (y[:, :wf] * (hdim ** -0.5 * LOG2E)).T
    spare = jnp.where(lax.broadcasted_iota(jnp.int32, (LANES - hdim, tm), 0) < N_FORGET_PIECES, -1.0, 0.0)
    qf_ref[...] = jnp.concatenate(
        [blk for hh in range(FOX_HEADS) for blk in (q_t[hh * hdim:(hh + 1) * hdim], spare)], axis=0).astype(BF16)
    ones_rows = jnp.where(lax.broadcasted_iota(jnp.int32, (ONES_ROWS, tm), 0) == 0, 1.0, 0.0)
    with_ones = lambda vt, dv: jnp.concatenate(
        [blk for hh in range(vt.shape[0] // dv) for blk in (vt[hh * dv:(hh + 1) * dv], ones_rows)], axis=0)
    vf_ref[...] = with_ones(y[:, 2 * wf:3 * wf].T, hdim).astype(BF16)
    o = 3 * wf
    c_q = y[:, o:o + MLA_Q_RANK]
    o += MLA_Q_RANK
    c_kv = y[:, o:o + MLA_KV_RANK]
    o += MLA_KV_RANK
    kp = y[:, o:o + LANES]
    f_logit = kp

    z = f_logit + fb_ref[...]
    log_f = jnp.minimum(z, 0.0) - jnp.log1p(jnp.exp(-jnp.abs(z)))

    @pl.when(pl.program_id(0) % tiles_per_seq == 0)
    def _():
        carry_ref[...] = jnp.zeros_like(carry_ref)

    f_cum = _row_scan(log_f, scan_steps) + carry_ref[...]
    carry_ref[...] = f_cum[-1:, :]
    lane = lax.broadcasted_iota(jnp.int32, (tm, LANES), 1)
    f_scaled = jnp.where(lane < FOX_HEADS, f_cum * LOG2E, 0.0)
    f_hi = f_scaled.astype(BF16).astype(F32)
    r1 = f_scaled - f_hi
    f_mid = r1.astype(BF16).astype(F32)
    f_lo = (r1 - f_mid).astype(BF16).astype(F32)
    pieces = f_hi + pltpu.roll(f_mid, FOX_HEADS, axis=1) + pltpu.roll(f_lo, 2 * FOX_HEADS, axis=1)
    placed = _dot(pieces.astype(BF16), place_ref[...])
    low = lane < hdim
    for j in range(wf // LANES):
        kk = y[:, wf + j * LANES:wf + (j + 1) * LANES]
        for hh, src in enumerate((kk, pltpu.roll(kk, hdim, axis=1))):
            cols = slice((2 * j + hh) * LANES, (2 * j + hh + 1) * LANES)
            kf_ref[:, cols] = jnp.where(low, src, placed[:, cols]).astype(BF16)

    nh = MLA_HEADS
    swap = lambda t: pltpu.roll(t, t.shape[1] - MLA_ROPE_DIM, axis=1)
    qb = _dot(_rms_rows(c_q, qn_ref[...]).astype(BF16), wq_ref[...])
    cq = jnp.concatenate([cq_ref[...]] * nh, axis=1)
    sq = jnp.concatenate([sq_ref[...]] * nh, axis=1)
    qm_ref[...] = (qb * cq + swap(qb) * sq).T.astype(BF16)
    kvb = _dot(_rms_rows(c_kv, kvn_ref[...]).astype(BF16), wkv_ref[...])
    k_rot = kp * ck_ref[...] + swap(kp) * sk_ref[...]
    km_ref[...] = (kvb[:, :nh * LANES] + jnp.concatenate([k_rot] * nh, axis=1)).astype(BF16)
    vm_ref[...] = with_ones(kvb[:, nh * LANES:].T, MLA_V_DIM).astype(BF16)


def _attn_in(x2, seq, g, w_in, f_bias, q_norm, w_uq, kv_norm, w_ukv):
    n, d = x2.shape
    tm = min(ROW_TILE, seq)
    wf = FOX_HEADS * FOX_HEAD_DIM
    half = MLA_ROPE_DIM // 2
    nh = MLA_HEADS

    qa, ka, va, fl, cq, ckv, kpe = jnp.split(
        w_in, np.cumsum([wf, wf, wf, FOX_HEADS, MLA_Q_RANK, MLA_KV_RANK]).tolist(), axis=1)
    z = lambda c: jnp.zeros((d, c), F32)
    p_blk = jnp.concatenate([fl, z(MLA_NOPE_DIM - FOX_HEADS), kpe, kpe[:, half:], kpe[:, :half]], axis=1)
    w1 = jnp.concatenate([qa, ka, va, cq, ckv, p_blk], axis=1).astype(BF16)
    n1 = w1.shape[1]
    fb = jnp.concatenate([f_bias, jnp.zeros((LANES - FOX_HEADS,), F32)]).reshape(1, LANES)
    n_piece = N_FORGET_PIECES
    src = (np.arange(n_piece)[:, None] * FOX_HEADS + np.arange(FOX_HEADS)[None, :]).ravel()
    dst = (np.arange(FOX_HEADS)[None, :] * LANES + FOX_HEAD_DIM + np.arange(n_piece)[:, None]).ravel()
    place_np = np.zeros((LANES, FOX_HEADS * LANES), np.float32)
    place_np[src, dst] = 1.0
    place = jnp.asarray(place_np, BF16)

    dq = MLA_NOPE_DIM + MLA_ROPE_DIM
    wq3 = w_uq.reshape(MLA_Q_RANK, nh, dq)
    wq = jnp.concatenate([wq3, wq3[:, :, MLA_NOPE_DIM + half:], wq3[:, :, MLA_NOPE_DIM:MLA_NOPE_DIM + half]],
                         axis=2).reshape(MLA_Q_RANK, nh * LANES).astype(BF16)
    wkv3 = w_ukv.reshape(MLA_KV_RANK, nh, MLA_NOPE_DIM + MLA_V_DIM)
    wk = jnp.concatenate([wkv3[:, :, :MLA_NOPE_DIM],
                          jnp.zeros((MLA_KV_RANK, nh, LANES - MLA_NOPE_DIM), F32)], axis=2)
    wkv = jnp.concatenate([wk.reshape(MLA_KV_RANK, nh * LANES),
                           wkv3[:, :, MLA_NOPE_DIM:].reshape(MLA_KV_RANK, nh * MLA_V_DIM)],
                          axis=1).astype(BF16)

    inv = ROPE_THETA ** (-jnp.arange(half, dtype=F32) / half)
    ang = jnp.arange(seq).astype(F32)[:, None] * inv[None, :]
    cos, sin = jnp.cos(ang), jnp.sin(ang)
    zl = lambda c: jnp.zeros((seq, c), F32)
    pad = LANES - dq
    c_tab = jnp.concatenate([zl(MLA_NOPE_DIM), cos, cos, zl(pad)], axis=1)
    s_tab = jnp.concatenate([zl(MLA_NOPE_DIM), -sin, sin, zl(pad)], axis=1)
    scale = dq ** -0.5 * LOG2E
    cq_tab = jnp.concatenate([jnp.full((seq, MLA_NOPE_DIM), scale, F32), scale * cos, scale * cos, zl(pad)], axis=1)
    sq_tab = scale * s_tab

    tps = seq // tm
    row = lambda w: pl.BlockSpec((tm, w), lambda i: (i, 0))
    tab = pl.BlockSpec((tm, LANES), lambda i: (i % tps, 0))
    col = lambda w: pl.BlockSpec((w, tm), lambda i: (0, i))
    outs = pl.pallas_call(
        functools.partial(_attn_in_body, tiles_per_seq=tps, scan_steps=int(np.log2(tm))),
        out_shape=[jax.ShapeDtypeStruct((FOX_HEADS * LANES, n), BF16), jax.ShapeDtypeStruct((n, FOX_HEADS * LANES), BF16),
                   jax.ShapeDtypeStruct((FOX_HEADS * (FOX_HEAD_DIM + ONES_ROWS), n), BF16),
                   jax.ShapeDtypeStruct((nh * LANES, n), BF16), jax.ShapeDtypeStruct((n, nh * LANES), BF16),
                   jax.ShapeDtypeStruct((nh * (MLA_V_DIM + ONES_ROWS), n), BF16)],
        grid=(n // tm,),
        in_specs=[row(d), _const_spec((1, d)), _const_spec((d, n1)), _const_spec((1, LANES)),
                  _const_spec((1, MLA_Q_RANK)), _const_spec(wq.shape),
                  _const_spec((1, MLA_KV_RANK)), _const_spec(wkv.shape), tab, tab, tab, tab,
                  _const_spec(place.shape)],
        out_specs=[col(FOX_HEADS * LANES), row(FOX_HEADS * LANES), col(FOX_HEADS * (FOX_HEAD_DIM + ONES_ROWS)),
                   col(nh * LANES), row(nh * LANES), col(nh * (MLA_V_DIM + ONES_ROWS))],
        scratch_shapes=[pltpu.VMEM((1, LANES), F32)],
        compiler_params=_params(("arbitrary",)),
        name="attn_in",
    )(x2, g.reshape(1, d), w1, fb, q_norm.reshape(1, -1), wq, kv_norm.reshape(1, -1), wkv,
      c_tab, s_tab, cq_tab, sq_tab, place)
    return outs


def _attn_body(qt_ref, k_ref, vt_ref, o_ref, *, hb, dq, dv, tq, chunk_mask):
    seq = k_ref.shape[1]
    dvp = dv + ONES_ROWS
    r = lax.broadcasted_iota(jnp.int32, (tq, tq), 0)
    c = lax.broadcasted_iota(jnp.int32, (tq, tq), 1)
    mask = (r // CHUNK <= c // CHUNK) if chunk_mask else (r <= c)

    def scores(i, hh):
        lo, hi = i * tq, (i + 1) * tq
        kl = slice(hh * dq, (hh + 1) * dq)
        qt = qt_ref[kl, lo:hi]
        s_d = jnp.where(mask, _dot(k_ref[0, lo:hi, kl], qt), NEG)
        m = jnp.max(s_d, axis=0, keepdims=True)
        s_m = None
        if i > 0:
            s_m = _dot(k_ref[0, :lo, kl], qt)
            m = jnp.maximum(m, jnp.max(s_m, axis=0, keepdims=True))
        p_d = jnp.exp2(s_d - m).astype(BF16)
        p_m = jnp.exp2(s_m - m).astype(BF16) if i > 0 else None
        return p_d, p_m

    def values(i, hh, p_d, p_m):
        lo, hi = i * tq, (i + 1) * tq
        vl = slice(hh * dvp, (hh + 1) * dvp)
        acc = _dot(vt_ref[vl, lo:hi], p_d)
        if i > 0:
            acc = acc + _dot(vt_ref[vl, :lo], p_m)
        return acc[:dv] / acc[dv:dv + 1]

    items = [(i, hh) for i in range(seq // tq) for hh in range(hb)]
    pending, outs = [], []
    for step in range(len(items) + ATTN_PIPELINE_DEPTH):
        if step < len(items):
            pending.append((items[step], scores(*items[step])))
        if step >= ATTN_PIPELINE_DEPTH:
            (i, hh), args = pending.pop(0)
            outs.append(values(i, hh, *args))
            if hh == hb - 1:
                o_ref[0, i * tq:(i + 1) * tq, :] = jnp.concatenate(outs, axis=0).T.astype(o_ref.dtype)
                outs = []


def _attention(qt, k, vt, n_heads, dq, dv, chunk_mask):
    b, seq, _ = k.shape
    hb = 2
    tq = min(ATTN_Q_TILE, seq)
    tok = lambda w: pl.BlockSpec((1, seq, hb * w), lambda bi, hi: (bi, 0, hi))
    feat = lambda w: pl.BlockSpec((hb * w, seq), lambda bi, hi: (hi, bi))
    return pl.pallas_call(
        functools.partial(_attn_body, hb=hb, dq=dq, dv=dv, tq=tq, chunk_mask=chunk_mask),
        out_shape=jax.ShapeDtypeStruct((b, seq, n_heads * dv), BF16),
        grid=(b, n_heads // hb),
        in_specs=[feat(dq), tok(dq), feat(dv + ONES_ROWS)],
        out_specs=tok(dv),
        compiler_params=_params(("parallel", "parallel")),
        name="mla_attn" if chunk_mask else "fox_attn",
    )(qt, k, vt)


def _gdn_in_body(x_ref, g_ref, w_ref, cw_ref, alog_ref, dt_ref,
                 q_ref, k_ref, kt_ref, v_ref, sg_ref, bg_ref, carry_ref,
                 *, tiles_per_seq, scan_steps, cc):
    hd = GDN_HEAD_DIM
    width = GDN_HEADS * hd
    tm = x_ref.shape[0]
    h = _rms_rows(x_ref[...], g_ref[...]).astype(BF16)

    @pl.when(pl.program_id(0) % tiles_per_seq == 0)
    def _():
        carry_ref[...] = jnp.zeros_like(carry_ref)

    nv = tm // SUBLANES
    sub = lax.broadcasted_iota(jnp.int32, (nv, SUBLANES, cc), 1)
    for part, out_ref in enumerate((q_ref, k_ref, v_ref)):
        for c in range(width // cc):
            cols = slice(part * width + c * cc, part * width + (c + 1) * cc)
            y = _dot(h, w_ref[:, cols])
            prev = carry_ref[:, cols]
            carry_ref[:, cols] = y[tm - SUBLANES:]
            acc = y * cw_ref[GDN_CONV - 1:GDN_CONV, cols]
            y3 = y.reshape(nv, SUBLANES, cc)
            for j in range(1, GDN_CONV):
                rot = pltpu.roll(y3, j, axis=1)
                rot_prev = jnp.concatenate([pltpu.roll(prev, j, axis=0)[None], rot[:-1]], axis=0)
                shifted = jnp.where(sub < j, rot_prev, rot).reshape(tm, cc)
                acc = acc + shifted * cw_ref[GDN_CONV - 1 - j:GDN_CONV - j, cols]
            a = acc * jax.nn.sigmoid(acc)
            if part < 2:
                heads = []
                for hh in range(cc // hd):
                    xh = a[:, hh * hd:(hh + 1) * hd]
                    xh = xh * lax.rsqrt(jnp.sum(xh * xh, axis=-1, keepdims=True) + 1e-6)
                    heads.append(xh * hd ** -0.5 if part == 0 else xh)
                a = jnp.concatenate(heads, axis=1)
            out_ref[:, c * cc:(c + 1) * cc] = a.astype(BF16)
            if part == 1:
                kt_ref[c * cc:(c + 1) * cc, :] = a.T.astype(BF16)
    for c in range(width // cc):
        gate = _dot(h, w_ref[:, 3 * width + c * cc:3 * width + (c + 1) * cc])
        sg_ref[:, c * cc:(c + 1) * cc] = (gate * jax.nn.sigmoid(gate)).astype(BF16)
    small = _dot(h, w_ref[:, 4 * width:])
    beta = jax.nn.sigmoid(small)
    z = small + dt_ref[...]
    softplus = jnp.maximum(z, 0.0) + jnp.log1p(jnp.exp(-jnp.abs(z)))
    g_log = -jnp.exp(alog_ref[...]) * softplus
    g_cum = _row_scan(g_log, scan_steps, period=CHUNK)
    lanes = lax.broadcasted_iota(jnp.int32, small.shape, 1)
    bg_ref[...] = jnp.where(lanes < GDN_HEADS, beta, g_cum).T[:2 * GDN_HEADS]


def _gdn_in(x2, seq, g, w_in, conv_w, a_log, dt_bias):
    n, d = x2.shape
    tm = min(ROW_TILE, seq)
    nhd = GDN_HEADS
    width = nhd * GDN_HEAD_DIM
    cc = min(GDN_COL_CHUNK, width)
    small = jnp.concatenate([w_in[:, 3 * width:3 * width + 2 * nhd], jnp.zeros((d, LANES - 2 * nhd), F32)], axis=1)
    w1 = jnp.concatenate([w_in[:, :3 * width], w_in[:, 3 * width + 2 * nhd:], small], axis=1).astype(BF16)
    lane_vec = lambda v: jnp.concatenate(
        [jnp.zeros((nhd,), F32), v, jnp.zeros((LANES - 2 * nhd,), F32)]).reshape(1, LANES)
    row = lambda w, dt: (jax.ShapeDtypeStruct((n, w), dt), pl.BlockSpec((tm, w), lambda i: (i, 0)))
    col = lambda w, dt: (jax.ShapeDtypeStruct((w, n), dt), pl.BlockSpec((w, tm), lambda i: (0, i)))
    outs = [row(width, BF16), row(width, BF16), col(width, BF16), row(width, BF16), row(width, BF16),
            col(2 * nhd, F32)]
    return pl.pallas_call(
        functools.partial(_gdn_in_body, tiles_per_seq=seq // tm, scan_steps=int(np.log2(CHUNK)), cc=cc),
        out_shape=[o[0] for o in outs],
        grid=(n // tm,),
        in_specs=[pl.BlockSpec((tm, d), lambda i: (i, 0)), _const_spec((1, d)), _const_spec(w1.shape),
                  _const_spec(conv_w.shape), _const_spec((1, LANES)), _const_spec((1, LANES))],
        out_specs=[o[1] for o in outs],
        scratch_shapes=[pltpu.VMEM((SUBLANES, 3 * width), F32)],
        compiler_params=_params(("arbitrary",)),
        name="gdn_in",
    )(x2, g.reshape(1, d), w1, conv_w, lane_vec(a_log), lane_vec(dt_bias))


def _gdn_body(q_ref, k_ref, kt_ref, v_ref, sg_ref, onorm_ref, g_ref, b_ref, gle_ref, glo_ref,
              o_ref, mq_ref, no_ref, *, hb, pairs_per_step):
    seq = q_ref.shape[1]
    hd = GDN_HEAD_DIM
    pair = 2 * CHUNK
    n_pairs = seq // pair
    ri = lax.broadcasted_iota(jnp.int32, (CHUNK, pair), 0)
    li = lax.broadcasted_iota(jnp.int32, (CHUNK, pair), 1)
    cj = li % CHUNK
    lane_lo = li < CHUNK
    row_lo = lax.broadcasted_iota(jnp.int32, (pair, pair), 0) < CHUNK
    eye = (ri == cj).astype(F32)

    def pack(x):
        return jnp.where(lane_lo, x[:CHUNK], x[CHUNK:])

    def block_diag(x):
        return jnp.concatenate([jnp.where(lane_lo, x, 0.0), jnp.where(lane_lo, 0.0, x)], axis=0).astype(BF16)

    def pair_stages(it):
        items = [(it * pairs_per_step + j, hh) for j in range(pairs_per_step) for hh in range(hb)]
        loaded = []
        for p, hh in items:
            rows = pl.ds(pl.multiple_of(p * pair, pair), pair)
            vec = pl.ds(p, 1)
            hc = slice(hh * hd, (hh + 1) * hd)
            gl_row = jnp.where(lane_lo[:1], gle_ref[0, hh, vec, :], glo_ref[0, hh, vec, :])
            loaded.append((k_ref[0, rows, hc], q_ref[0, rows, hc], v_ref[0, rows, hc], kt_ref[hc, rows],
                           g_ref[0, hh, vec, :], b_ref[0, hh, vec, :], gl_row))
        heads = range(len(loaded))
        k2, q2, v2, kt2, g_row, b_row, gl_row = zip(*loaded)
        g_col = [jnp.broadcast_to(g_row[h], (pair, pair)).T for h in heads]
        b_col = [jnp.broadcast_to(b_row[h], (pair, pair)).T for h in heads]
        decay = [jnp.where(ri >= cj, jnp.exp(pack(g_col[h]) - g_row[h]), 0.0) for h in heads]
        gram = [_dot(jnp.concatenate([k2[h][:CHUNK], q2[h][:CHUNK], k2[h][CHUNK:], q2[h][CHUNK:]], axis=0),
                     kt2[h]) for h in heads]
        kk = [jnp.where(lane_lo, gram[h][:CHUNK], gram[h][2 * CHUNK:3 * CHUNK]) for h in heads]
        qk = [jnp.where(lane_lo, gram[h][CHUNK:2 * CHUNK], gram[h][3 * CHUNK:]) for h in heads]
        pw = [jnp.where(ri > cj, -(kk[h] * pack(b_col[h]) * decay[h]), 0.0) for h in heads]
        intra = [qk[h] * decay[h] for h in heads]
        yield
        t_mat = [eye + pw[h] for h in heads]
        n_sq = int(np.log2(CHUNK))
        for it in range(n_sq):
            rhs = [block_diag(pw[h]) for h in heads]
            if it == 0:
                pw = [_dot(pw[h].astype(BF16), rhs[h]) for h in heads]
            elif it < n_sq - 1:
                res = [_dot(jnp.concatenate([pw[h], t_mat[h]], axis=0).astype(BF16), rhs[h]) for h in heads]
                pw = [res[h][:CHUNK] for h in heads]
                t_mat = [t_mat[h] + res[h][CHUNK:] for h in heads]
            else:
                t_mat = [t_mat[h] + _dot(t_mat[h].astype(BF16), rhs[h]) for h in heads]
            yield
        ub =[_dot(block_diag(t_mat[h] * b_row[h]), v2[h]).astype(BF16) for h in heads]
        wb = [_dot(block_diag(t_mat[h] * (b_row[h] * jnp.exp(g_row[h]))), k2[h]).astype(BF16) for h in heads]
        yield
        kt_t = [(kt2[h].astype(F32) * jnp.exp(gl_row[h] - g_row[h])).astype(BF16) for h in heads]
        zero = jnp.zeros_like(wb[0])
        rhs4 = [jnp.concatenate([jnp.where(row_lo, wb[h], zero), jnp.where(row_lo, ub[h], zero),
                                 jnp.where(row_lo, zero, wb[h]), jnp.where(row_lo, zero, ub[h])], axis=1)
                for h in heads]
        mn = [_dot(kt_t[h], rhs4[h]) for h in heads]
        iwu = [_dot(block_diag(intra[h]), jnp.concatenate([wb[h], ub[h]], axis=1)) for h in heads]
        q_eff = [q2[h].astype(F32) * jnp.exp(g_col[h]) - iwu[h][:, :hd] for h in heads]
        yield
        return items, list(zip(mn, iwu, q_eff))

    def store_pairs(items, results):
        for (p, hh), (mn, iwu, q_eff) in zip(items, results):
            for half in range(2):
                ch = 2 * p + half
                rs = slice(half * CHUNK, (half + 1) * CHUNK)
                mq_ref[hh, ch, :hd, :] = mn[:, 2 * half * hd:(2 * half + 1) * hd].astype(BF16)
                mq_ref[hh, ch, hd:, :] = q_eff[rs].astype(BF16)
                no_ref[hh, ch, :hd, :] = mn[:, (2 * half + 1) * hd:(2 * half + 2) * hd]
                no_ref[hh, ch, hd:, :] = iwu[rs, hd:]

    def scan_stages(it, states):
        for j in range(pairs_per_step):
            p = it * pairs_per_step + j
            for half, gl_ref in enumerate((gle_ref, glo_ref)):
                ch = 2 * p + half
                rows = pl.ds(pl.multiple_of(ch * CHUNK, CHUNK), CHUNK)
                res = [_dot(mq_ref[hh, ch], states[hh].astype(BF16)) for hh in range(hb)]
                outs = []
                for hh in range(hb):
                    outs.append(res[hh][hd:] + no_ref[hh, ch, hd:, :])
                    states[hh] = (states[hh] * jnp.exp(gl_ref[0, hh, pl.ds(p, 1), :]) - res[hh][:hd]
                                  + no_ref[hh, ch, :hd, :])
                for hh in range(hb):
                    hc = slice(hh * hd, (hh + 1) * hd)
                    o_ref[0, rows, hc] = (_rms_rows(outs[hh], onorm_ref[...])
                                          * sg_ref[0, rows, hc].astype(F32)).astype(o_ref.dtype)
                yield

    def run(pair_gen, scan_gen):
        stored = None
        while pair_gen is not None or scan_gen is not None:
            if pair_gen is not None:
                try:
                    next(pair_gen)
                except StopIteration as done:
                    stored, pair_gen = done.value, None
            if scan_gen is not None:
                try:
                    next(scan_gen)
                except StopIteration:
                    scan_gen = None
        if stored is not None:
            store_pairs(*stored)

    n_steps = n_pairs // pairs_per_step
    run(pair_stages(0), None)

    def merged_step(it, states):
        states = list(states)
        run(pair_stages(it), scan_stages(it - 1, states))
        return tuple(states)

    states = lax.fori_loop(1, n_steps, merged_step, tuple(jnp.zeros((hd, hd), F32) for _ in range(hb)))
    run(None, scan_stages(n_steps - 1, list(states)))


def _gdn(q, k, kt, v, sg, out_norm, bg):
    b, seq, _ = q.shape
    nh, hd = GDN_HEADS, GDN_HEAD_DIM
    hb = GDN_HEAD_BLOCK
    pair = 2 * CHUNK
    n_pairs = seq // pair
    bg = bg.reshape(2, nh, b, n_pairs, pair).transpose(0, 2, 1, 3, 4)
    b_rows, g_rows = bg[0], bg[1]
    g_last = g_rows.reshape(b, nh, n_pairs, 2, CHUNK)[..., CHUNK - 1]
    gl_even = jnp.broadcast_to(g_last[..., 0:1], (b, nh, n_pairs, pair))
    gl_odd = jnp.broadcast_to(g_last[..., 1:2], (b, nh, n_pairs, pair))
    tok = pl.BlockSpec((1, seq, hb * hd), lambda bi, hi: (bi, 0, hi))
    vec = pl.BlockSpec((1, hb, n_pairs, pair), lambda bi, hi: (bi, hi, 0, 0))
    return pl.pallas_call(
        functools.partial(_gdn_body, hb=hb, pairs_per_step=GDN_PAIRS_PER_STEP),
        out_shape=jax.ShapeDtypeStruct((b, seq, nh * hd), BF16),
        grid=(b, nh // hb),
        in_specs=[tok, tok, pl.BlockSpec((hb * hd, seq), lambda bi, hi: (hi, bi)), tok, tok,
                  pl.BlockSpec((1, hd), lambda bi, hi: (0, 0)), vec, vec, vec, vec],
        out_specs=tok,
        scratch_shapes=[pltpu.VMEM((hb, seq // CHUNK, hd + CHUNK, hd), BF16),
                        pltpu.VMEM((hb, seq // CHUNK, hd + CHUNK, hd), F32)],
        compiler_params=_params(("parallel", "parallel")),
        name="gdn",
    )(q, k, kt, v, sg, out_norm.reshape(1, hd), g_rows, b_rows, gl_even, gl_odd)


def kernel(x, ffn1_norm, ffn1_w13, ffn1_w2, mix_norm, attn_w_in, fox_f_bias, mla_q_norm, mla_w_uq,
           mla_kv_norm, mla_w_ukv, attn_w_out, gdn_w_in, gdn_conv_w, gdn_a_log, gdn_dt_bias,
           gdn_out_norm, gdn_w_out, ffn2_norm, ffn2_w13, ffn2_w2, final_norm):
    b, seq, d = x.shape
    n = b * seq
    depth = ffn1_norm.shape[0]
    x2 = x.reshape(n, d)
    w13_a, w2_a, w13_b, w2_b = (w.astype(BF16) for w in (ffn1_w13, ffn1_w2, ffn2_w13, ffn2_w2))
    for layer in range(depth):
        x2 = _ffn(x2, layer, ffn1_norm, w13_a, w2_a)
        i = layer // 2
        sh = lambda t: t.reshape(b, seq, -1)
        if layer % 2 == 0:
            qf, kf, vf, qm, km, vm = _attn_in(
                x2, seq, mix_norm[layer], attn_w_in[i], fox_f_bias[i], mla_q_norm[i], mla_w_uq[i],
                mla_kv_norm[i], mla_w_ukv[i])
            o_a = _attention(qf, sh(kf), vf, FOX_HEADS, LANES, FOX_HEAD_DIM, False)
            o_b = _attention(qm, sh(km), vm, MLA_HEADS, LANES, MLA_V_DIM, True)
            wf = FOX_HEADS * FOX_HEAD_DIM
            w_out = attn_w_out[i].astype(BF16)
            mix, mix_w = [o_a.reshape(n, -1), o_b.reshape(n, -1)], [w_out[:wf], w_out[wf:]]
        else:
            q, k, kt, v, sg, bg = _gdn_in(x2, seq, mix_norm[layer], gdn_w_in[i], gdn_conv_w[i],
                                          gdn_a_log[i], gdn_dt_bias[i])
            o = _gdn(sh(q), sh(k), kt, sh(v), sh(sg), gdn_out_norm[i], bg)
            mix, mix_w = [o.reshape(n, -1)], [gdn_w_out[i].astype(BF16)]
        x2 = _ffn(x2, layer, ffn2_norm, w13_b, w2_b, mix, mix_w, final_norm if layer == depth - 1 else None)
    return x2.reshape(b, seq, d)
```

```python
import functools

import numpy as np
import jax
import jax.numpy as jnp
from jax import lax
from jax.experimental import pallas as pl
from jax.experimental.pallas import tpu as pltpu

F32 = jnp.float32
BF16 = jnp.bfloat16

CHUNK = 64
RMS_EPS = 1e-6
FOX_HEADS = 8
FOX_HEAD_DIM = 64
MLA_HEADS = 8
MLA_Q_RANK = 256
MLA_KV_RANK = 128
MLA_NOPE_DIM = 64
MLA_ROPE_DIM = 32
MLA_V_DIM = 64
ROPE_THETA = 10000.0
GDN_HEADS = 8
GDN_HEAD_DIM = 128
GDN_CONV = 4

LANES = 128
SUBLANES = 8
VMEM_LIMIT = 56 * 1024 * 1024
ROW_TILE = 512
ATTN_Q_TILE = 256
ATTN_PIPELINE_DEPTH = 4
GDN_HEAD_BLOCK = 4
GDN_PAIRS_PER_STEP = 4
GDN_COL_CHUNK = 512
NEG = float(jnp.finfo(jnp.float32).min)
LOG2E = float(np.log2(np.e))
ONES_ROWS = 16
N_FORGET_PIECES = 3


def _params(sem):
    return pltpu.CompilerParams(dimension_semantics=sem, vmem_limit_bytes=VMEM_LIMIT)


def _rms_rows(x, g):
    return x * lax.rsqrt(jnp.mean(x * x, axis=-1, keepdims=True) + RMS_EPS) * g


def _dot(a, b):
    return jnp.dot(a, b, preferred_element_type=F32)


def _dot_nt(a, b):
    return lax.dot_general(a, b, (((1,), (1,)), ((), ())), preferred_element_type=F32)


def _const_spec(shape):
    return pl.BlockSpec(shape, lambda *_: (0,) * len(shape), pipeline_mode=pl.Buffered(1))


def _row_scan(y, n_steps, period=None):
    rows = lax.broadcasted_iota(jnp.int32, y.shape, 0)
    if period is not None:
        rows = rows % period
    shift = 1
    for _ in range(n_steps):
        y = y + jnp.where(rows >= shift, pltpu.roll(y, shift, axis=0), 0.0)
        shift *= 2
    return y


def _alternate(*gens):
    gens = list(gens)
    while gens:
        for gen in list(gens):
            try:
                next(gen)
            except StopIteration:
                gens.remove(gen)


def _ffn_body(*refs, d_ff, n_chunks, n_mix, final, follow):
    x_ref, refs = refs[0], refs[1:]
    mix_refs, wo_refs, refs = refs[:n_mix], refs[n_mix:2 * n_mix], refs[2 * n_mix:]
    g_ref, w13_ref, w2_ref = refs[:3]
    refs = refs[3:]
    fg_ref = None
    if final:
        fg_ref, refs = refs[0], refs[1:]
    if follow is not None:
        fg2_ref, refs = refs[0], refs[1:]
        f_in, refs = refs[:follow["n_in"]], refs[follow["n_in"]:]
        o_ref, refs = refs[0], refs[1:]
        f_out, refs = refs[:follow["n_out"]], refs[follow["n_out"]:]
        t_ref, hprev_ref, f_scratch = refs[0], refs[1], refs[2:]
        step = pl.program_id(0)

        @pl.when(step == 0)
        def _():
            hprev_ref[...] = jnp.zeros_like(hprev_ref)

        follow["prepare"](step - 1, f_scratch)
    else:
        o_ref, t_ref = refs

    def ffn_stages():
        x = x_ref[...]
        for m_ref, wo_ref in zip(mix_refs, wo_refs):
            x = x + _dot(m_ref[...], wo_ref[...])
        h = _rms_rows(x, g_ref[...]).astype(BF16)
        fc = d_ff // n_chunks
        for j in range(n_chunks):
            a = _dot(h, w13_ref[:, j * fc:(j + 1) * fc])
            b = _dot(h, w13_ref[:, d_ff + j * fc:d_ff + (j + 1) * fc])
            t_ref[:, j * fc:(j + 1) * fc] = (a * jax.nn.sigmoid(a) * b).astype(BF16)
            yield
        out = x + 0.5 * _dot(t_ref[...], w2_ref[...])
        if follow is not None:
            hprev_ref[...] = _rms_rows(out, fg2_ref[...]).astype(BF16)
        if final:
            out = _rms_rows(out, fg_ref[...])
        o_ref[...] = out

    if follow is None:
        _alternate(ffn_stages())
    else:
        _alternate(follow["stages"](hprev_ref[...], f_in, f_out, f_scratch), ffn_stages())


def _ffn(x2, layer, g, w13, w2, mix=(), w_out=(), final_g=None, follow=None):
    n, d = x2.shape
    d_ff = w2.shape[1]
    tm = min(ROW_TILE, n)
    n_tiles = n // tm
    final = final_g is not None
    cur = (lambda i: jnp.minimum(i, n_tiles - 1)) if follow is not None else (lambda i: i)
    row = lambda w: pl.BlockSpec((tm, w), lambda i: (cur(i), 0))
    layer_spec = lambda *shape: pl.BlockSpec((None,) + shape, lambda i: (layer,) + (0,) * len(shape),
                                             pipeline_mode=pl.Buffered(1))
    args = [x2, *mix, *w_out, g.reshape(-1, 1, d), w13, w2]
    specs = ([row(d)] + [row(m.shape[1]) for m in mix] + [_const_spec(w.shape) for w in w_out]
             + [layer_spec(1, d), layer_spec(d, 2 * d_ff), layer_spec(d_ff, d)])
    if final:
        args.append(final_g.reshape(1, d))
        specs.append(_const_spec((1, d)))
    out_shape, out_specs = [jax.ShapeDtypeStruct((n, d), F32)], [row(d)]
    scratch = [pltpu.VMEM((tm, d_ff), BF16)]
    body_follow = None
    if follow is not None:
        args += [follow["norm_g"].reshape(1, d), *follow["args"]]
        specs += [_const_spec((1, d)), *follow["in_specs"]]
        out_shape += follow["out_shape"]
        out_specs += follow["out_specs"]
        scratch += [pltpu.VMEM((tm, d), BF16), *follow["scratch"]]
        body_follow = dict(n_in=len(follow["args"]), n_out=len(follow["out_shape"]),
                           prepare=follow["prepare"], stages=follow["stages"])
    outs = pl.pallas_call(
        functools.partial(_ffn_body, d_ff=d_ff, n_chunks=d_ff // 256, n_mix=len(mix), final=final,
                          follow=body_follow),
        out_shape=out_shape,
        grid=(n_tiles + (follow is not None),),
        in_specs=specs,
        out_specs=out_specs,
        scratch_shapes=scratch,
        compiler_params=_params(("parallel",) if follow is None else ("arbitrary",)),
        name="ffn" if follow is None else "ffn_" + follow["name"],
    )(*args)
    return outs[0] if follow is None else (outs[0], outs[1:])


def _attn_in_body(x_ref, g_ref, w_ref, fb_ref, qn_ref, wq_ref, kvn_ref, wkv_ref,
                  ck_ref, sk_ref, cq_ref, sq_ref, place_ref,
                  qf_ref, kf_ref, vf_ref, qm_ref, km_ref, vm_ref, carry_ref,
                  *, tiles_per_seq, scan_steps):
    wf = FOX_HEADS * FOX_HEAD_DIM
    hdim = FOX_HEAD_DIM
    tm = x_ref.shape[0]
    h = _rms_rows(x_ref[...], g_ref[...]).astype(BF16)
    y = _dot(h, w_ref[...])
    q_t = (y[:, :wf] * (hdim ** -0.5 * LOG2E)).T
    spare = jnp.where(lax.broadcasted_iota(jnp.int32, (LANES - hdim, tm), 0) < N_FORGET_PIECES, -1.0, 0.0)
    qf_ref[...] = jnp.concatenate(
        [blk for hh in range(FOX_HEADS) for blk in (q_t[hh * hdim:(hh + 1) * hdim], spare)], axis=0).astype(BF16)
    ones_rows = jnp.where(lax.broadcasted_iota(jnp.int32, (ONES_ROWS, tm), 0) == 0, 1.0, 0.0)
    with_ones = lambda vt, dv: jnp.concatenate(
        [blk for hh in range(vt.shape[0] // dv) for blk in (vt[hh * dv:(hh + 1) * dv], ones_rows)], axis=0)
    vf_ref[...] = with_ones(y[:, 2 * wf:3 * wf].T, hdim).astype(BF16)
    o = 3 * wf
    c_q = y[:, o:o + MLA_Q_RANK]
    o += MLA_Q_RANK
    c_kv = y[:, o:o + MLA_KV_RANK]
    o += MLA_KV_RANK
    kp = y[:, o:o + LANES]
    f_logit = kp

    z = f_logit + fb_ref[...]
    log_f = jnp.minimum(z, 0.0) - jnp.log1p(jnp.exp(-jnp.abs(z)))

    @pl.when(pl.program_id(0) % tiles_per_seq == 0)
    def _():
        carry_ref[...] = jnp.zeros_like(carry_ref)

    f_cum = _row_scan(log_f, scan_steps) + carry_ref[...]
    carry_ref[...] = f_cum[-1:, :]
    lane = lax.broadcasted_iota(jnp.int32, (tm, LANES), 1)
    f_scaled = jnp.where(lane < FOX_HEADS, f_cum * LOG2E, 0.0)
    f_hi = f_scaled.astype(BF16).astype(F32)
    r1 = f_scaled - f_hi
    f_mid = r1.astype(BF16).astype(F32)
    f_lo = (r1 - f_mid).astype(BF16).astype(F32)
    pieces = f_hi + pltpu.roll(f_mid, FOX_HEADS, axis=1) + pltpu.roll(f_lo, 2 * FOX_HEADS, axis=1)
    placed = _dot(pieces.astype(BF16), place_ref[...])
    low = lane < hdim
    for j in range(wf // LANES):
        kk = y[:, wf + j * LANES:wf + (j + 1) * LANES]
        for hh, src in enumerate((kk, pltpu.roll(kk, hdim, axis=1))):
            cols = slice((2 * j + hh) * LANES, (2 * j + hh + 1) * LANES)
            kf_ref[:, cols] = jnp.where(low, src, placed[:, cols]).astype(BF16)

    nh = MLA_HEADS
    swap = lambda t: pltpu.roll(t, t.shape[1] - MLA_ROPE_DIM, axis=1)
    qb = _dot(_rms_rows(c_q, qn_ref[...]).astype(BF16), wq_ref[...])
    cq = jnp.concatenate([cq_ref[...]] * nh, axis=1)
    sq = jnp.concatenate([sq_ref[...]] * nh, axis=1)
    qm_ref[...] = (qb * cq + swap(qb) * sq).T.astype(BF16)
    kvb = _dot(_rms_rows(c_kv, kvn_ref[...]).astype(BF16), wkv_ref[...])
    k_rot = kp * ck_ref[...] + swap(kp) * sk_ref[...]
    km_ref[...] = (kvb[:, :nh * LANES] + jnp.concatenate([k_rot] * nh, axis=1)).astype(BF16)
    vm_ref[...] = with_ones(kvb[:, nh * LANES:].T, MLA_V_DIM).astype(BF16)


def _attn_in(x2, seq, g, w_in, f_bias, q_norm, w_uq, kv_norm, w_ukv):
    n, d = x2.shape
    tm = min(ROW_TILE, seq)
    wf = FOX_HEADS * FOX_HEAD_DIM
    half = MLA_ROPE_DIM // 2
    nh = MLA_HEADS

    qa, ka, va, fl, cq, ckv, kpe = jnp.split(
        w_in, np.cumsum([wf, wf, wf, FOX_HEADS, MLA_Q_RANK, MLA_KV_RANK]).tolist(), axis=1)
    z = lambda c: jnp.zeros((d, c), F32)
    p_blk = jnp.concatenate([fl, z(MLA_NOPE_DIM - FOX_HEADS), kpe, kpe[:, half:], kpe[:, :half]], axis=1)
    w1 = jnp.concatenate([qa, ka, va, cq, ckv, p_blk], axis=1).astype(BF16)
    n1 = w1.shape[1]
    fb = jnp.concatenate([f_bias, jnp.zeros((LANES - FOX_HEADS,), F32)]).reshape(1, LANES)
    n_piece = N_FORGET_PIECES
    src = (np.arange(n_piece)[:, None] * FOX_HEADS + np.arange(FOX_HEADS)[None, :]).ravel()
    dst = (np.arange(FOX_HEADS)[None, :] * LANES + FOX_HEAD_DIM + np.arange(n_piece)[:, None]).ravel()
    place_np = np.zeros((LANES, FOX_HEADS * LANES), np.float32)
    place_np[src, dst] = 1.0
    place = jnp.asarray(place_np, BF16)

    dq = MLA_NOPE_DIM + MLA_ROPE_DIM
    wq3 = w_uq.reshape(MLA_Q_RANK, nh, dq)
    wq = jnp.concatenate([wq3, wq3[:, :, MLA_NOPE_DIM + half:], wq3[:, :, MLA_NOPE_DIM:MLA_NOPE_DIM + half]],
                         axis=2).reshape(MLA_Q_RANK, nh * LANES).astype(BF16)
    wkv3 = w_ukv.reshape(MLA_KV_RANK, nh, MLA_NOPE_DIM + MLA_V_DIM)
    wk = jnp.concatenate([wkv3[:, :, :MLA_NOPE_DIM],
                          jnp.zeros((MLA_KV_RANK, nh, LANES - MLA_NOPE_DIM), F32)], axis=2)
    wkv = jnp.concatenate([wk.reshape(MLA_KV_RANK, nh * LANES),
                           wkv3[:, :, MLA_NOPE_DIM:].reshape(MLA_KV_RANK, nh * MLA_V_DIM)],
                          axis=1).astype(BF16)

    inv = ROPE_THETA ** (-jnp.arange(half, dtype=F32) / half)
    ang = jnp.arange(seq).astype(F32)[:, None] * inv[None, :]
    cos, sin = jnp.cos(ang), jnp.sin(ang)
    zl = lambda c: jnp.zeros((seq, c), F32)
    pad = LANES - dq
    c_tab = jnp.concatenate([zl(MLA_NOPE_DIM), cos, cos, zl(pad)], axis=1)
    s_tab = jnp.concatenate([zl(MLA_NOPE_DIM), -sin, sin, zl(pad)], axis=1)
    scale = dq ** -0.5 * LOG2E
    cq_tab = jnp.concatenate([jnp.full((seq, MLA_NOPE_DIM), scale, F32), scale * cos, scale * cos, zl(pad)], axis=1)
    sq_tab = scale * s_tab

    tps = seq // tm
    row = lambda w: pl.BlockSpec((tm, w), lambda i: (i, 0))
    tab = pl.BlockSpec((tm, LANES), lambda i: (i % tps, 0))
    col = lambda w: pl.BlockSpec((w, tm), lambda i: (0, i))
    outs = pl.pallas_call(
        functools.partial(_attn_in_body, tiles_per_seq=tps, scan_steps=int(np.log2(tm))),
        out_shape=[jax.ShapeDtypeStruct((FOX_HEADS * LANES, n), BF16), jax.ShapeDtypeStruct((n, FOX_HEADS * LANES), BF16),
                   jax.ShapeDtypeStruct((FOX_HEADS * (FOX_HEAD_DIM + ONES_ROWS), n), BF16),
                   jax.ShapeDtypeStruct((nh * LANES, n), BF16), jax.ShapeDtypeStruct((n, nh * LANES), BF16),
                   jax.ShapeDtypeStruct((nh * (MLA_V_DIM + ONES_ROWS), n), BF16)],
        grid=(n // tm,),
        in_specs=[row(d), _const_spec((1, d)), _const_spec((d, n1)), _const_spec((1, LANES)),
                  _const_spec((1, MLA_Q_RANK)), _const_spec(wq.shape),
                  _const_spec((1, MLA_KV_RANK)), _const_spec(wkv.shape), tab, tab, tab, tab,
                  _const_spec(place.shape)],
        out_specs=[col(FOX_HEADS * LANES), row(FOX_HEADS * LANES), col(FOX_HEADS * (FOX_HEAD_DIM + ONES_ROWS)),
                   col(nh * LANES), row(nh * LANES), col(nh * (MLA_V_DIM + ONES_ROWS))],
        scratch_shapes=[pltpu.VMEM((1, LANES), F32)],
        compiler_params=_params(("arbitrary",)),
        name="attn_in",
    )(x2, g.reshape(1, d), w1, fb, q_norm.reshape(1, -1), wq, kv_norm.reshape(1, -1), wkv,
      c_tab, s_tab, cq_tab, sq_tab, place)
    return outs


def _attn_body(qt_ref, k_ref, vt_ref, o_ref, *, hb, dq, dv, tq, chunk_mask):
    seq = k_ref.shape[1]
    dvp = dv + ONES_ROWS
    r = lax.broadcasted_iota(jnp.int32, (tq, tq), 0)
    c = lax.broadcasted_iota(jnp.int32, (tq, tq), 1)
    mask = (r // CHUNK <= c // CHUNK) if chunk_mask else (r <= c)

    def scores(i, hh):
        lo, hi = i * tq, (i + 1) * tq
        kl = slice(hh * dq, (hh + 1) * dq)
        qt = qt_ref[kl, lo:hi]
        s_d = jnp.where(mask, _dot(k_ref[0, lo:hi, kl], qt), NEG)
        m = jnp.max(s_d, axis=0, keepdims=True)
        s_m = None
        if i > 0:
            s_m = _dot(k_ref[0, :lo, kl], qt)
            m = jnp.maximum(m, jnp.max(s_m, axis=0, keepdims=True))
        p_d = jnp.exp2(s_d - m).astype(BF16)
        p_m = jnp.exp2(s_m - m).astype(BF16) if i > 0 else None
        return p_d, p_m

    def values(i, hh, p_d, p_m):
        lo, hi = i * tq, (i + 1) * tq
        vl = slice(hh * dvp, (hh + 1) * dvp)
        acc = _dot(vt_ref[vl, lo:hi], p_d)
        if i > 0:
            acc = acc + _dot(vt_ref[vl, :lo], p_m)
        return acc[:dv] / acc[dv:dv + 1]

    items = [(i, hh) for i in range(seq // tq) for hh in range(hb)]
    pending, outs = [], []
    for step in range(len(items) + ATTN_PIPELINE_DEPTH):
        if step < len(items):
            pending.append((items[step], scores(*items[step])))
        if step >= ATTN_PIPELINE_DEPTH:
            (i, hh), args = pending.pop(0)
            outs.append(values(i, hh, *args))
            if hh == hb - 1:
                o_ref[0, i * tq:(i + 1) * tq, :] = jnp.concatenate(outs, axis=0).T.astype(o_ref.dtype)
                outs = []


def _attention(qt, k, vt, n_heads, dq, dv, chunk_mask):
    b, seq, _ = k.shape
    hb = 2
    tq = min(ATTN_Q_TILE, seq)
    tok = lambda w: pl.BlockSpec((1, seq, hb * w), lambda bi, hi: (bi, 0, hi))
    feat = lambda w: pl.BlockSpec((hb * w, seq), lambda bi, hi: (hi, bi))
    return pl.pallas_call(
        functools.partial(_attn_body, hb=hb, dq=dq, dv=dv, tq=tq, chunk_mask=chunk_mask),
        out_shape=jax.ShapeDtypeStruct((b, seq, n_heads * dv), BF16),
        grid=(b, n_heads // hb),
        in_specs=[feat(dq), tok(dq), feat(dv + ONES_ROWS)],
        out_specs=tok(dv),
        compiler_params=_params(("parallel", "parallel")),
        name="mla_attn" if chunk_mask else "fox_attn",
    )(qt, k, vt)


def _gdn_in_prepare(tile, scratch, *, tiles_per_seq):
    carry_ref, = scratch

    @pl.when(tile % tiles_per_seq == 0)
    def _():
        carry_ref[...] = jnp.zeros_like(carry_ref)


def _gdn_in_stages(h, ins, outs, scratch, *, scan_steps, cc):
    w_ref, cw_ref, alog_ref, dt_ref = ins
    q_ref, k_ref, kt_ref, v_ref, sg_ref, bg_ref = outs
    carry_ref, = scratch
    hd = GDN_HEAD_DIM
    width = GDN_HEADS * hd
    tm = h.shape[0]
    nv = tm // SUBLANES
    sub = lax.broadcasted_iota(jnp.int32, (nv, SUBLANES, cc), 1)
    for part, out_ref in enumerate((q_ref, k_ref, v_ref)):
        for c in range(width // cc):
            cols = slice(part * width + c * cc, part * width + (c + 1) * cc)
            y = _dot(h, w_ref[:, cols])
            prev = carry_ref[:, cols]
            carry_ref[:, cols] = y[tm - SUBLANES:]
            acc = y * cw_ref[GDN_CONV - 1:GDN_CONV, cols]
            y3 = y.reshape(nv, SUBLANES, cc)
            for j in range(1, GDN_CONV):
                rot = pltpu.roll(y3, j, axis=1)
                rot_prev = jnp.concatenate([pltpu.roll(prev, j, axis=0)[None], rot[:-1]], axis=0)
                shifted = jnp.where(sub < j, rot_prev, rot).reshape(tm, cc)
                acc = acc + shifted * cw_ref[GDN_CONV - 1 - j:GDN_CONV - j, cols]
            a = acc * jax.nn.sigmoid(acc)
            if part < 2:
                heads = []
                for hh in range(cc // hd):
                    xh = a[:, hh * hd:(hh + 1) * hd]
                    xh = xh * lax.rsqrt(jnp.sum(xh * xh, axis=-1, keepdims=True) + 1e-6)
                    heads.append(xh * hd ** -0.5 if part == 0 else xh)
                a = jnp.concatenate(heads, axis=1)
            out_ref[:, c * cc:(c + 1) * cc] = a.astype(BF16)
            if part == 1:
                kt_ref[c * cc:(c + 1) * cc, :] = a.T.astype(BF16)
            yield
    for c in range(width // cc):
        gate = _dot(h, w_ref[:, 3 * width + c * cc:3 * width + (c + 1) * cc])
        sg_ref[:, c * cc:(c + 1) * cc] = (gate * jax.nn.sigmoid(gate)).astype(BF16)
        yield
    small = _dot(h, w_ref[:, 4 * width:])
    beta = jax.nn.sigmoid(small)
    z = small + dt_ref[...]
    softplus = jnp.maximum(z, 0.0) + jnp.log1p(jnp.exp(-jnp.abs(z)))
    g_log = -jnp.exp(alog_ref[...]) * softplus
    g_cum = _row_scan(g_log, scan_steps, period=CHUNK)
    lanes = lax.broadcasted_iota(jnp.int32, small.shape, 1)
    bg_ref[...] = jnp.where(lanes < GDN_HEADS, beta, g_cum).T[:2 * GDN_HEADS]


def _gdn_in_follow(n, d, seq, g, w_in, conv_w, a_log, dt_bias):
    tm = min(ROW_TILE, seq)
    nhd = GDN_HEADS
    width = nhd * GDN_HEAD_DIM
    cc = min(GDN_COL_CHUNK, width)
    small = jnp.concatenate([w_in[:, 3 * width:3 * width + 2 * nhd], jnp.zeros((d, LANES - 2 * nhd), F32)], axis=1)
    w1 = jnp.concatenate([w_in[:, :3 * width], w_in[:, 3 * width + 2 * nhd:], small], axis=1).astype(BF16)
    lane_vec = lambda v: jnp.concatenate(
        [jnp.zeros((nhd,), F32), v, jnp.zeros((LANES - 2 * nhd,), F32)]).reshape(1, LANES)
    prev = lambda i: jnp.maximum(i - 1, 0)
    row = lambda w, dt: (jax.ShapeDtypeStruct((n, w), dt), pl.BlockSpec((tm, w), lambda i: (prev(i), 0)))
    col = lambda w, dt: (jax.ShapeDtypeStruct((w, n), dt), pl.BlockSpec((w, tm), lambda i: (0, prev(i))))
    outs = [row(width, BF16), row(width, BF16), col(width, BF16), row(width, BF16), row(width, BF16),
            col(2 * nhd, F32)]
    return dict(
        name="gdn_in", norm_g=g,
        args=[w1, conv_w, lane_vec(a_log), lane_vec(dt_bias)],
        in_specs=[_const_spec(w1.shape), _const_spec(conv_w.shape), _const_spec((1, LANES)), _const_spec((1, LANES))],
        out_shape=[o[0] for o in outs], out_specs=[o[1] for o in outs],
        scratch=[pltpu.VMEM((SUBLANES, 3 * width), F32)],
        prepare=functools.partial(_gdn_in_prepare, tiles_per_seq=seq // tm),
        stages=functools.partial(_gdn_in_stages, scan_steps=int(np.log2(CHUNK)), cc=cc))


def _gdn_body(q_ref, k_ref, kt_ref, v_ref, sg_ref, onorm_ref, g_ref, b_ref, gle_ref, glo_ref,
              o_ref, mq_ref, no_ref, *, hb, pairs_per_step):
    seq = q_ref.shape[1]
    hd = GDN_HEAD_DIM
    pair = 2 * CHUNK
    n_pairs = seq // pair
    ri = lax.broadcasted_iota(jnp.int32, (CHUNK, pair), 0)
    li = lax.broadcasted_iota(jnp.int32, (CHUNK, pair), 1)
    cj = li % CHUNK
    lane_lo = li < CHUNK
    row_lo = lax.broadcasted_iota(jnp.int32, (pair, pair), 0) < CHUNK
    eye = (ri == cj).astype(F32)

    def pack(x):
        return jnp.where(lane_lo, x[:CHUNK], x[CHUNK:])

    def block_diag(x):
        return jnp.concatenate([jnp.where(lane_lo, x, 0.0), jnp.where(lane_lo, 0.0, x)], axis=0).astype(BF16)

    def pair_stages(it):
        items = [(it * pairs_per_step + j, hh) for j in range(pairs_per_step) for hh in range(hb)]
        loaded = []
        for p, hh in items:
            rows = pl.ds(pl.multiple_of(p * pair, pair), pair)
            vec = pl.ds(p, 1)
            hc = slice(hh * hd, (hh + 1) * hd)
            gl_row = jnp.where(lane_lo[:1], gle_ref[0, hh, vec, :], glo_ref[0, hh, vec, :])
            loaded.append((k_ref[0, rows, hc], q_ref[0, rows, hc], v_ref[0, rows, hc], kt_ref[hc, rows],
                           g_ref[0, hh, vec, :], b_ref[0, hh, vec, :], gl_row))
        heads = range(len(loaded))
        k2, q2, v2, kt2, g_row, b_row, gl_row = zip(*loaded)
        g_col = [jnp.broadcast_to(g_row[h], (pair, pair)).T for h in heads]
        b_col = [jnp.broadcast_to(b_row[h], (pair, pair)).T for h in heads]
        decay = [jnp.where(ri >= cj, jnp.exp(pack(g_col[h]) - g_row[h]), 0.0) for h in heads]
        gram = [_dot(jnp.concatenate([k2[h][:CHUNK], q2[h][:CHUNK], k2[h][CHUNK:], q2[h][CHUNK:]], axis=0),
                     kt2[h]) for h in heads]
        kk = [jnp.where(lane_lo, gram[h][:CHUNK], gram[h][2 * CHUNK:3 * CHUNK]) for h in heads]
        qk = [jnp.where(lane_lo, gram[h][CHUNK:2 * CHUNK], gram[h][3 * CHUNK:]) for h in heads]
        pw = [jnp.where(ri > cj, -(kk[h] * pack(b_col[h]) * decay[h]), 0.0) for h in heads]
        intra = [qk[h] * decay[h] for h in heads]
        yield
        t_mat = [eye + pw[h] for h in heads]
        n_sq = int(np.log2(CHUNK))
        for it in range(n_sq):
            rhs = [block_diag(pw[h]) for h in heads]
            if it == 0:
                pw = [_dot(pw[h].astype(BF16), rhs[h]) for h in heads]
            elif it < n_sq - 1:
                res = [_dot(jnp.concatenate([pw[h], t_mat[h]], axis=0).astype(BF16), rhs[h]) for h in heads]
                pw = [res[h][:CHUNK] for h in heads]
                t_mat = [t_mat[h] + res[h][CHUNK:] for h in heads]
            else:
                t_mat = [t_mat[h] + _dot(t_mat[h].astype(BF16), rhs[h]) for h in heads]
            yield
        ub =[_dot(block_diag(t_mat[h] * b_row[h]), v2[h]).astype(BF16) for h in heads]
        wb = [_dot(block_diag(t_mat[h] * (b_row[h] * jnp.exp(g_row[h]))), k2[h]).astype(BF16) for h in heads]
        yield
        kt_t = [(kt2[h].astype(F32) * jnp.exp(gl_row[h] - g_row[h])).astype(BF16) for h in heads]
        zero = jnp.zeros_like(wb[0])
        rhs4 = [jnp.concatenate([jnp.where(row_lo, wb[h], zero), jnp.where(row_lo, ub[h], zero),
                                 jnp.where(row_lo, zero, wb[h]), jnp.where(row_lo, zero, ub[h])], axis=1)
                for h in heads]
        mn = [_dot(kt_t[h], rhs4[h]) for h in heads]
        iwu = [_dot(block_diag(intra[h]), jnp.concatenate([wb[h], ub[h]], axis=1)) for h in heads]
        q_eff = [q2[h].astype(F32) * jnp.exp(g_col[h]) - iwu[h][:, :hd] for h in heads]
        yield
        return items, list(zip(mn, iwu, q_eff))

    def store_pairs(items, results):
        for (p, hh), (mn, iwu, q_eff) in zip(items, results):
            for half in range(2):
                ch = 2 * p + half
                rs = slice(half * CHUNK, (half + 1) * CHUNK)
                mq_ref[hh, ch, :hd, :] = mn[:, 2 * half * hd:(2 * half + 1) * hd].astype(BF16)
                mq_ref[hh, ch, hd:, :] = q_eff[rs].astype(BF16)
                no_ref[hh, ch, :hd, :] = mn[:, (2 * half + 1) * hd:(2 * half + 2) * hd]
                no_ref[hh, ch, hd:, :] = iwu[rs, hd:]

    def scan_stages(it, states):
        for j in range(pairs_per_step):
            p = it * pairs_per_step + j
            for half, gl_ref in enumerate((gle_ref, glo_ref)):
                ch = 2 * p + half
                rows = pl.ds(pl.multiple_of(ch * CHUNK, CHUNK), CHUNK)
                res = [_dot(mq_ref[hh, ch], states[hh].astype(BF16)) for hh in range(hb)]
                outs = []
                for hh in range(hb):
                    outs.append(res[hh][hd:] + no_ref[hh, ch, hd:, :])
                    states[hh] = (states[hh] * jnp.exp(gl_ref[0, hh, pl.ds(p, 1), :]) - res[hh][:hd]
                                  + no_ref[hh, ch, :hd, :])
                for hh in range(hb):
                    hc = slice(hh * hd, (hh + 1) * hd)
                    o_ref[0, rows, hc] = (_rms_rows(outs[hh], onorm_ref[...])
                                          * sg_ref[0, rows, hc].astype(F32)).astype(o_ref.dtype)
                yield

    def run(pair_gen, scan_gen):
        stored = None
        while pair_gen is not None or scan_gen is not None:
            if pair_gen is not None:
                try:
                    next(pair_gen)
                except StopIteration as done:
                    stored, pair_gen = done.value, None
            if scan_gen is not None:
                try:
                    next(scan_gen)
                except StopIteration:
                    scan_gen = None
        if stored is not None:
            store_pairs(*stored)

    n_steps = n_pairs // pairs_per_step
    run(pair_stages(0), None)

    def merged_step(it, states):
        states = list(states)
        run(pair_stages(it), scan_stages(it - 1, states))
        return tuple(states)

    states = lax.fori_loop(1, n_steps, merged_step, tuple(jnp.zeros((hd, hd), F32) for _ in range(hb)))
    run(None, scan_stages(n_steps - 1, list(states)))


def _gdn(q, k, kt, v, sg, out_norm, bg):
    b, seq, _ = q.shape
    nh, hd = GDN_HEADS, GDN_HEAD_DIM
    hb = GDN_HEAD_BLOCK
    pair = 2 * CHUNK
    n_pairs = seq // pair
    bg = bg.reshape(2, nh, b, n_pairs, pair).transpose(0, 2, 1, 3, 4)
    b_rows, g_rows = bg[0], bg[1]
    g_last = g_rows.reshape(b, nh, n_pairs, 2, CHUNK)[..., CHUNK - 1]
    gl_even = jnp.broadcast_to(g_last[..., 0:1], (b, nh, n_pairs, pair))
    gl_odd = jnp.broadcast_to(g_last[..., 1:2], (b, nh, n_pairs, pair))
    tok = pl.BlockSpec((1, seq, hb * hd), lambda bi, hi: (bi, 0, hi))
    vec = pl.BlockSpec((1, hb, n_pairs, pair), lambda bi, hi: (bi, hi, 0, 0))
    return pl.pallas_call(
        functools.partial(_gdn_body, hb=hb, pairs_per_step=GDN_PAIRS_PER_STEP),
        out_shape=jax.ShapeDtypeStruct((b, seq, nh * hd), BF16),
        grid=(b, nh // hb),
        in_specs=[tok, tok, pl.BlockSpec((hb * hd, seq), lambda bi, hi: (hi, bi)), tok, tok,
                  pl.BlockSpec((1, hd), lambda bi, hi: (0, 0)), vec, vec, vec, vec],
        out_specs=tok,
        scratch_shapes=[pltpu.VMEM((hb, seq // CHUNK, hd + CHUNK, hd), BF16),
                        pltpu.VMEM((hb, seq // CHUNK, hd + CHUNK, hd), F32)],
        compiler_params=_params(("parallel", "parallel")),
        name="gdn",
    )(q, k, kt, v, sg, out_norm.reshape(1, hd), g_rows, b_rows, gl_even, gl_odd)


def kernel(x, ffn1_norm, ffn1_w13, ffn1_w2, mix_norm, attn_w_in, fox_f_bias, mla_q_norm, mla_w_uq,
           mla_kv_norm, mla_w_ukv, attn_w_out, gdn_w_in, gdn_conv_w, gdn_a_log, gdn_dt_bias,
           gdn_out_norm, gdn_w_out, ffn2_norm, ffn2_w13, ffn2_w2, final_norm):
    b, seq, d = x.shape
    n = b * seq
    depth = ffn1_norm.shape[0]
    x2 = x.reshape(n, d)
    w13_a, w2_a, w13_b, w2_b = (w.astype(BF16) for w in (ffn1_w13, ffn1_w2, ffn2_w13, ffn2_w2))
    for layer in range(depth):
        i = layer // 2
        sh = lambda t: t.reshape(b, seq, -1)
        if layer % 2 == 0:
            x2 = _ffn(x2, layer, ffn1_norm, w13_a, w2_a)
            qf, kf, vf, qm, km, vm = _attn_in(
                x2, seq, mix_norm[layer], attn_w_in[i], fox_f_bias[i], mla_q_norm[i], mla_w_uq[i],
                mla_kv_norm[i], mla_w_ukv[i])
            o_a = _attention(qf, sh(kf), vf, FOX_HEADS, LANES, FOX_HEAD_DIM, False)
            o_b = _attention(qm, sh(km), vm, MLA_HEADS, LANES, MLA_V_DIM, True)
            wf = FOX_HEADS * FOX_HEAD_DIM
            w_out = attn_w_out[i].astype(BF16)
            mix, mix_w = [o_a.reshape(n, -1), o_b.reshape(n, -1)], [w_out[:wf], w_out[wf:]]
        else:
            x2, (q, k, kt, v, sg, bg) = _ffn(
                x2, layer, ffn1_norm, w13_a, w2_a,
                follow=_gdn_in_follow(n, d, seq, mix_norm[layer], gdn_w_in[i], gdn_conv_w[i],
                                      gdn_a_log[i], gdn_dt_bias[i]))
            o = _gdn(sh(q), sh(k), kt, sh(v), sh(sg), gdn_out_norm[i], bg)
            mix, mix_w = [o.reshape(n, -1)], [gdn_w_out[i].astype(BF16)]
        x2 = _ffn(x2, layer, ffn2_norm, w13_b, w2_b, mix, mix_w, final_norm if layer == depth - 1 else None)
    return x2.reshape(b, seq, d)
```

```python
import functools

import numpy as np
import jax
import jax.numpy as jnp
from jax import lax
from jax.experimental import pallas as pl
from jax.experimental.pallas import tpu as pltpu

F32 = jnp.float32
BF16 = jnp.bfloat16

CHUNK = 64
RMS_EPS = 1e-6
FOX_HEADS = 8
FOX_HEAD_DIM = 64
MLA_HEADS = 8
MLA_Q_RANK = 256
MLA_KV_RANK = 128
MLA_NOPE_DIM = 64
MLA_ROPE_DIM = 32
MLA_V_DIM = 64
ROPE_THETA = 10000.0
GDN_HEADS = 8
GDN_HEAD_DIM = 128
GDN_CONV = 4

LANES = 128
SUBLANES = 8
VMEM_LIMIT = 56 * 1024 * 1024
ROW_TILE = 512
FFN_ROW_TILE = 1024
ATTN_Q_TILE = 256
ATTN_HEAD_BLOCK = 2
ATTN_PIPELINE_DEPTH = 8
GDN_HEAD_BLOCK = 4
GDN_PAIRS_PER_STEP = 4
GDN_COL_CHUNK = 512
NEG = float(jnp.finfo(jnp.float32).min)
LOG2E = float(np.log2(np.e))
ONES_ROWS = 16
N_FORGET_PIECES = 3


def _params(sem):
    return pltpu.CompilerParams(dimension_semantics=sem, vmem_limit_bytes=VMEM_LIMIT)


def _rms_rows(x, g):
    return x * lax.rsqrt(jnp.mean(x * x, axis=-1, keepdims=True) + RMS_EPS) * g


def _dot(a, b):
    return jnp.dot(a, b, preferred_element_type=F32)


def _dot_nt(a, b):
    return lax.dot_general(a, b, (((1,), (1,)), ((), ())), preferred_element_type=F32)


def _const_spec(shape):
    return pl.BlockSpec(shape, lambda *_: (0,) * len(shape), pipeline_mode=pl.Buffered(1))


def _row_scan(y, n_steps, period=None):
    rows = lax.broadcasted_iota(jnp.int32, y.shape, 0)
    if period is not None:
        rows = rows % period
    shift = 1
    for _ in range(n_steps):
        y = y + jnp.where(rows >= shift, pltpu.roll(y, shift, axis=0), 0.0)
        shift *= 2
    return y


def _alternate(*gens):
    gens = list(gens)
    while gens:
        for gen in list(gens):
            try:
                next(gen)
            except StopIteration:
                gens.remove(gen)


def _ffn_body(*refs, d_ff, n_chunks, n_mix, final, follow):
    x_ref, refs = refs[0], refs[1:]
    mix_refs, wo_refs, refs = refs[:n_mix], refs[n_mix:2 * n_mix], refs[2 * n_mix:]
    g_ref, w13_ref, w2_ref = refs[:3]
    refs = refs[3:]
    fg_ref = None
    if final:
        fg_ref, refs = refs[0], refs[1:]
    if follow is not None:
        fg2_ref, refs = refs[0], refs[1:]
        f_in, refs = refs[:follow["n_in"]], refs[follow["n_in"]:]
        o_ref, refs = refs[0], refs[1:]
        f_out, refs = refs[:follow["n_out"]], refs[follow["n_out"]:]
        t_ref, hprev_ref, f_scratch = refs[0], refs[1], refs[2:]
        step = pl.program_id(0)

        @pl.when(step == 0)
        def _():
            hprev_ref[...] = jnp.zeros_like(hprev_ref)

        follow["prepare"](step - 1, f_scratch)
    else:
        o_ref, t_ref = refs

    def ffn_stages():
        x = x_ref[...]
        for m_ref, wo_ref in zip(mix_refs, wo_refs):
            x = x + _dot(m_ref[...], wo_ref[...])
        h = _rms_rows(x, g_ref[...]).astype(BF16)
        fc = d_ff // n_chunks
        for j in range(n_chunks):
            a = _dot(h, w13_ref[:, j * fc:(j + 1) * fc])
            b = _dot(h, w13_ref[:, d_ff + j * fc:d_ff + (j + 1) * fc])
            t_ref[:, j * fc:(j + 1) * fc] = (a * jax.nn.sigmoid(a) * b).astype(BF16)
            yield
        out = x + 0.5 * _dot(t_ref[...], w2_ref[...])
        if follow is not None:
            hprev_ref[...] = _rms_rows(out, fg2_ref[...]).astype(BF16)
        if final:
            out = _rms_rows(out, fg_ref[...])
        o_ref[...] = out

    if follow is None:
        _alternate(ffn_stages())
    else:
        _alternate(follow["stages"](hprev_ref[...], f_in, f_out, f_scratch), ffn_stages())


def _ffn(x2, layer, g, w13, w2, mix=(), w_out=(), final_g=None, follow=None):
    n, d = x2.shape
    d_ff = w2.shape[1]
    tm = min(ROW_TILE if follow is not None else FFN_ROW_TILE, n)
    n_tiles = n // tm
    final = final_g is not None
    cur = (lambda i: jnp.minimum(i, n_tiles - 1)) if follow is not None else (lambda i: i)
    row = lambda w: pl.BlockSpec((tm, w), lambda i: (cur(i), 0))
    layer_spec = lambda *shape: pl.BlockSpec((None,) + shape, lambda i: (layer,) + (0,) * len(shape),
                                             pipeline_mode=pl.Buffered(1))
    args = [x2, *mix, *w_out, g.reshape(-1, 1, d), w13, w2]
    specs = ([row(d)] + [row(m.shape[1]) for m in mix] + [_const_spec(w.shape) for w in w_out]
             + [layer_spec(1, d), layer_spec(d, 2 * d_ff), layer_spec(d_ff, d)])
    if final:
        args.append(final_g.reshape(1, d))
        specs.append(_const_spec((1, d)))
    out_shape, out_specs = [jax.ShapeDtypeStruct((n, d), F32)], [row(d)]
    scratch = [pltpu.VMEM((tm, d_ff), BF16)]
    body_follow = None
    if follow is not None:
        args += [follow["norm_g"].reshape(1, d), *follow["args"]]
        specs += [_const_spec((1, d)), *follow["in_specs"]]
        out_shape += follow["out_shape"]
        out_specs += follow["out_specs"]
        scratch += [pltpu.VMEM((tm, d), BF16), *follow["scratch"]]
        body_follow = dict(n_in=len(follow["args"]), n_out=len(follow["out_shape"]),
                           prepare=follow["prepare"], stages=follow["stages"])
    outs = pl.pallas_call(
        functools.partial(_ffn_body, d_ff=d_ff, n_chunks=d_ff // 256, n_mix=len(mix), final=final,
                          follow=body_follow),
        out_shape=out_shape,
        grid=(n_tiles + (follow is not None),),
        in_specs=specs,
        out_specs=out_specs,
        scratch_shapes=scratch,
        compiler_params=_params(("parallel",) if follow is None else ("arbitrary",)),
        name="ffn" if follow is None else "ffn_" + follow["name"],
    )(*args)
    return outs[0] if follow is None else (outs[0], outs[1:])


def _attn_in_body(x_ref, g_ref, w_ref, fb_ref, qn_ref, wq_ref, kvn_ref, wkv_ref,
                  ck_ref, sk_ref, cq_ref, sq_ref, place_ref,
                  qf_ref, kf_ref, vf_ref, qm_ref, km_ref, vm_ref, carry_ref,
                  *, tiles_per_seq, scan_steps):
    wf = FOX_HEADS * FOX_HEAD_DIM
    hdim = FOX_HEAD_DIM
    tm = x_ref.shape[0]
    h = _rms_rows(x_ref[...], g_ref[...]).astype(BF16)
    y = _dot(h, w_ref[...])
    q_t = (y[:, :wf] * (hdim ** -0.5 * LOG2E)).T
    spare = jnp.where(lax.broadcasted_iota(jnp.int32, (LANES - hdim, tm), 0) < N_FORGET_PIECES, -1.0, 0.0)
    qf_ref[...] = jnp.concatenate(
        [blk for hh in range(FOX_HEADS) for blk in (q_t[hh * hdim:(hh + 1) * hdim], spare)], axis=0).astype(BF16)
    ones_rows = jnp.where(lax.broadcasted_iota(jnp.int32, (ONES_ROWS, tm), 0) == 0, 1.0, 0.0)
    with_ones = lambda vt, dv: jnp.concatenate(
        [blk for hh in range(vt.shape[0] // dv) for blk in (vt[hh * dv:(hh + 1) * dv], ones_rows)], axis=0)
    vf_ref[...] = with_ones(y[:, 2 * wf:3 * wf].T, hdim).astype(BF16)
    o = 3 * wf
    c_q = y[:, o:o + MLA_Q_RANK]
    o += MLA_Q_RANK
    c_kv = y[:, o:o + MLA_KV_RANK]
    o += MLA_KV_RANK
    kp = y[:, o:o + LANES]
    f_logit = kp

    z = f_logit + fb_ref[...]
    log_f = jnp.minimum(z, 0.0) - jnp.log1p(jnp.exp(-jnp.abs(z)))

    @pl.when(pl.program_id(0) % tiles_per_seq == 0)
    def _():
        carry_ref[...] = jnp.zeros_like(carry_ref)

    f_cum = _row_scan(log_f, scan_steps) + carry_ref[...]
    carry_ref[...] = f_cum[-1:, :]
    lane = lax.broadcasted_iota(jnp.int32, (tm, LANES), 1)
    f_scaled = jnp.where(lane < FOX_HEADS, f_cum * LOG2E, 0.0)
    f_hi = f_scaled.astype(BF16).astype(F32)
    r1 = f_scaled - f_hi
    f_mid = r1.astype(BF16).astype(F32)
    f_lo = (r1 - f_mid).astype(BF16).astype(F32)
    pieces = f_hi + pltpu.roll(f_mid, FOX_HEADS, axis=1) + pltpu.roll(f_lo, 2 * FOX_HEADS, axis=1)
    placed = _dot(pieces.astype(BF16), place_ref[...])
    low = lane < hdim
    for j in range(wf // LANES):
        kk = y[:, wf + j * LANES:wf + (j + 1) * LANES]
        for hh, src in enumerate((kk, pltpu.roll(kk, hdim, axis=1))):
            cols = slice((2 * j + hh) * LANES, (2 * j + hh + 1) * LANES)
            kf_ref[:, cols] = jnp.where(low, src, placed[:, cols]).astype(BF16)

    nh = MLA_HEADS
    swap = lambda t: pltpu.roll(t, t.shape[1] - MLA_ROPE_DIM, axis=1)
    qb = _dot(_rms_rows(c_q, qn_ref[...]).astype(BF16), wq_ref[...])
    cq = jnp.concatenate([cq_ref[...]] * nh, axis=1)
    sq = jnp.concatenate([sq_ref[...]] * nh, axis=1)
    qm_ref[...] = (qb * cq + swap(qb) * sq).T.astype(BF16)
    kvb = _dot(_rms_rows(c_kv, kvn_ref[...]).astype(BF16), wkv_ref[...])
    k_rot = kp * ck_ref[...] + swap(kp) * sk_ref[...]
    km_ref[...] = (kvb[:, :nh * LANES] + jnp.concatenate([k_rot] * nh, axis=1)).astype(BF16)
    vm_ref[...] = with_ones(kvb[:, nh * LANES:].T, MLA_V_DIM).astype(BF16)


def _attn_in(x2, seq, g, w_in, f_bias, q_norm, w_uq, kv_norm, w_ukv):
    n, d = x2.shape
    tm = min(ROW_TILE, seq)
    wf = FOX_HEADS * FOX_HEAD_DIM
    half = MLA_ROPE_DIM // 2
    nh = MLA_HEADS

    qa, ka, va, fl, cq, ckv, kpe = jnp.split(
        w_in, np.cumsum([wf, wf, wf, FOX_HEADS, MLA_Q_RANK, MLA_KV_RANK]).tolist(), axis=1)
    z = lambda c: jnp.zeros((d, c), F32)
    p_blk = jnp.concatenate([fl, z(MLA_NOPE_DIM - FOX_HEADS), kpe, kpe[:, half:], kpe[:, :half]], axis=1)
    w1 = jnp.concatenate([qa, ka, va, cq, ckv, p_blk], axis=1).astype(BF16)
    n1 = w1.shape[1]
    fb = jnp.concatenate([f_bias, jnp.zeros((LANES - FOX_HEADS,), F32)]).reshape(1, LANES)
    n_piece = N_FORGET_PIECES
    src = (np.arange(n_piece)[:, None] * FOX_HEADS + np.arange(FOX_HEADS)[None, :]).ravel()
    dst = (np.arange(FOX_HEADS)[None, :] * LANES + FOX_HEAD_DIM + np.arange(n_piece)[:, None]).ravel()
    place_np = np.zeros((LANES, FOX_HEADS * LANES), np.float32)
    place_np[src, dst] = 1.0
    place = jnp.asarray(place_np, BF16)

    dq = MLA_NOPE_DIM + MLA_ROPE_DIM
    wq3 = w_uq.reshape(MLA_Q_RANK, nh, dq)
    wq = jnp.concatenate([wq3, wq3[:, :, MLA_NOPE_DIM + half:], wq3[:, :, MLA_NOPE_DIM:MLA_NOPE_DIM + half]],
                         axis=2).reshape(MLA_Q_RANK, nh * LANES).astype(BF16)
    wkv3 = w_ukv.reshape(MLA_KV_RANK, nh, MLA_NOPE_DIM + MLA_V_DIM)
    wk = jnp.concatenate([wkv3[:, :, :MLA_NOPE_DIM],
                          jnp.zeros((MLA_KV_RANK, nh, LANES - MLA_NOPE_DIM), F32)], axis=2)
    wkv = jnp.concatenate([wk.reshape(MLA_KV_RANK, nh * LANES),
                           wkv3[:, :, MLA_NOPE_DIM:].reshape(MLA_KV_RANK, nh * MLA_V_DIM)],
                          axis=1).astype(BF16)

    inv = ROPE_THETA ** (-jnp.arange(half, dtype=F32) / half)
    ang = jnp.arange(seq).astype(F32)[:, None] * inv[None, :]
    cos, sin = jnp.cos(ang), jnp.sin(ang)
    zl = lambda c: jnp.zeros((seq, c), F32)
    pad = LANES - dq
    c_tab = jnp.concatenate([zl(MLA_NOPE_DIM), cos, cos, zl(pad)], axis=1)
    s_tab = jnp.concatenate([zl(MLA_NOPE_DIM), -sin, sin, zl(pad)], axis=1)
    scale = dq ** -0.5 * LOG2E
    cq_tab = jnp.concatenate([jnp.full((seq, MLA_NOPE_DIM), scale, F32), scale * cos, scale * cos, zl(pad)], axis=1)
    sq_tab = scale * s_tab

    tps = seq // tm
    row = lambda w: pl.BlockSpec((tm, w), lambda i: (i, 0))
    tab = pl.BlockSpec((tm, LANES), lambda i: (i % tps, 0))
    col = lambda w: pl.BlockSpec((w, tm), lambda i: (0, i))
    outs = pl.pallas_call(
        functools.partial(_attn_in_body, tiles_per_seq=tps, scan_steps=int(np.log2(tm))),
        out_shape=[jax.ShapeDtypeStruct((FOX_HEADS * LANES, n), BF16), jax.ShapeDtypeStruct((n, FOX_HEADS * LANES), BF16),
                   jax.ShapeDtypeStruct((FOX_HEADS * (FOX_HEAD_DIM + ONES_ROWS), n), BF16),
                   jax.ShapeDtypeStruct((nh * LANES, n), BF16), jax.ShapeDtypeStruct((n, nh * LANES), BF16),
                   jax.ShapeDtypeStruct((nh * (MLA_V_DIM + ONES_ROWS), n), BF16)],
        grid=(n // tm,),
        in_specs=[row(d), _const_spec((1, d)), _const_spec((d, n1)), _const_spec((1, LANES)),
                  _const_spec((1, MLA_Q_RANK)), _const_spec(wq.shape),
                  _const_spec((1, MLA_KV_RANK)), _const_spec(wkv.shape), tab, tab, tab, tab,
                  _const_spec(place.shape)],
        out_specs=[col(FOX_HEADS * LANES), row(FOX_HEADS * LANES), col(FOX_HEADS * (FOX_HEAD_DIM + ONES_ROWS)),
                   col(nh * LANES), row(nh * LANES), col(nh * (MLA_V_DIM + ONES_ROWS))],
        scratch_shapes=[pltpu.VMEM((1, LANES), F32)],
        compiler_params=_params(("arbitrary",)),
        name="attn_in",
    )(x2, g.reshape(1, d), w1, fb, q_norm.reshape(1, -1), wq, kv_norm.reshape(1, -1), wkv,
      c_tab, s_tab, cq_tab, sq_tab, place)
    return outs


def _attn_body(qt_ref, k_ref, vt_ref, o_ref, *, hb, dq, dv, tq, chunk_mask):
    seq = k_ref.shape[1]
    dvp = dv + ONES_ROWS
    r = lax.broadcasted_iota(jnp.int32, (tq, tq), 0)
    c = lax.broadcasted_iota(jnp.int32, (tq, tq), 1)
    mask = (r // CHUNK <= c // CHUNK) if chunk_mask else (r <= c)

    def scores(i, hh, j):
        kl = slice(hh * dq, (hh + 1) * dq)
        s = _dot(k_ref[0, j * tq:(j + 1) * tq, kl], qt_ref[kl, i * tq:(i + 1) * tq])
        if j == i:
            s = jnp.where(mask, s, NEG)
        m = jnp.max(s, axis=0, keepdims=True)
        return jnp.exp2(s - m).astype(BF16), m

    def values(hh, j, p):
        return _dot(vt_ref[hh * dvp:(hh + 1) * dvp, j * tq:(j + 1) * tq], p)

    def combine(parts):
        m_all = functools.reduce(jnp.maximum, [m for m, _ in parts])
        acc = sum(o * jnp.exp2(m - m_all) for m, o in parts)
        return acc[:dv] / acc[dv:dv + 1]

    units = [(i, hh, j) for i in range(seq // tq) for hh in range(hb) for j in range(i + 1)]
    pending, parts, outs = [], [], []
    for step in range(len(units) + ATTN_PIPELINE_DEPTH):
        if step < len(units):
            pending.append((units[step], scores(*units[step])))
        if step >= ATTN_PIPELINE_DEPTH:
            (i, hh, j), (p, m) = pending.pop(0)
            parts.append((m, values(hh, j, p)))
            if j == i:
                outs.append(combine(parts))
                parts = []
                if hh == hb - 1:
                    o_ref[0, i * tq:(i + 1) * tq, :] = jnp.concatenate(outs, axis=0).T.astype(o_ref.dtype)
                    outs = []


def _attention(qt, k, vt, n_heads, dq, dv, chunk_mask):
    b, seq, _ = k.shape
    hb = ATTN_HEAD_BLOCK
    tq = min(ATTN_Q_TILE, seq)
    tok = lambda w: pl.BlockSpec((1, seq, hb * w), lambda bi, hi: (bi, 0, hi))
    feat = lambda w: pl.BlockSpec((hb * w, seq), lambda bi, hi: (hi, bi))
    return pl.pallas_call(
        functools.partial(_attn_body, hb=hb, dq=dq, dv=dv, tq=tq, chunk_mask=chunk_mask),
        out_shape=jax.ShapeDtypeStruct((b, seq, n_heads * dv), BF16),
        grid=(b, n_heads // hb),
        in_specs=[feat(dq), tok(dq), feat(dv + ONES_ROWS)],
        out_specs=tok(dv),
        compiler_params=_params(("parallel", "parallel")),
        name="mla_attn" if chunk_mask else "fox_attn",
    )(qt, k, vt)


def _gdn_in_prepare(tile, scratch, *, tiles_per_seq):
    carry_ref, = scratch

    @pl.when(tile % tiles_per_seq == 0)
    def _():
        carry_ref[...] = jnp.zeros_like(carry_ref)


def _gdn_in_stages(h, ins, outs, scratch, *, scan_steps, cc):
    w_ref, cw_ref, alog_ref, dt_ref = ins
    q_ref, k_ref, kt_ref, v_ref, sg_ref, bg_ref = outs
    carry_ref, = scratch
    hd = GDN_HEAD_DIM
    width = GDN_HEADS * hd
    tm = h.shape[0]
    nv = tm // SUBLANES
    sub = lax.broadcasted_iota(jnp.int32, (nv, SUBLANES, cc), 1)
    for part, out_ref in enumerate((q_ref, k_ref, v_ref)):
        for c in range(width // cc):
            cols = slice(part * width + c * cc, part * width + (c + 1) * cc)
            y = _dot(h, w_ref[:, cols])
            prev = carry_ref[:, cols]
            carry_ref[:, cols] = y[tm - SUBLANES:]
            acc = y * cw_ref[GDN_CONV - 1:GDN_CONV, cols]
            y3 = y.reshape(nv, SUBLANES, cc)
            for j in range(1, GDN_CONV):
                rot = pltpu.roll(y3, j, axis=1)
                rot_prev = jnp.concatenate([pltpu.roll(prev, j, axis=0)[None], rot[:-1]], axis=0)
                shifted = jnp.where(sub < j, rot_prev, rot).reshape(tm, cc)
                acc = acc + shifted * cw_ref[GDN_CONV - 1 - j:GDN_CONV - j, cols]
            a = acc * jax.nn.sigmoid(acc)
            if part < 2:
                heads = []
                for hh in range(cc // hd):
                    xh = a[:, hh * hd:(hh + 1) * hd]
                    xh = xh * lax.rsqrt(jnp.sum(xh * xh, axis=-1, keepdims=True) + 1e-6)
                    heads.append(xh * hd ** -0.5 if part == 0 else xh)
                a = jnp.concatenate(heads, axis=1)
            out_ref[:, c * cc:(c + 1) * cc] = a.astype(BF16)
            if part == 1:
                kt_ref[c * cc:(c + 1) * cc, :] = a.T.astype(BF16)
            yield
    for c in range(width // cc):
        gate = _dot(h, w_ref[:, 3 * width + c * cc:3 * width + (c + 1) * cc])
        sg_ref[:, c * cc:(c + 1) * cc] = (gate * jax.nn.sigmoid(gate)).astype(BF16)
        yield
    small = _dot(h, w_ref[:, 4 * width:])
    beta = jax.nn.sigmoid(small)
    z = small + dt_ref[...]
    softplus = jnp.maximum(z, 0.0) + jnp.log1p(jnp.exp(-jnp.abs(z)))
    g_log = -jnp.exp(alog_ref[...]) * softplus
    g_cum = _row_scan(g_log, scan_steps, period=CHUNK)
    lanes = lax.broadcasted_iota(jnp.int32, small.shape, 1)
    bg_ref[...] = jnp.where(lanes < GDN_HEADS, beta, g_cum).T[:2 * GDN_HEADS]


def _gdn_in_follow(n, d, seq, g, w_in, conv_w, a_log, dt_bias):
    tm = min(ROW_TILE, seq)
    nhd = GDN_HEADS
    width = nhd * GDN_HEAD_DIM
    cc = min(GDN_COL_CHUNK, width)
    small = jnp.concatenate([w_in[:, 3 * width:3 * width + 2 * nhd], jnp.zeros((d, LANES - 2 * nhd), F32)], axis=1)
    w1 = jnp.concatenate([w_in[:, :3 * width], w_in[:, 3 * width + 2 * nhd:], small], axis=1).astype(BF16)
    lane_vec = lambda v: jnp.concatenate(
        [jnp.zeros((nhd,), F32), v, jnp.zeros((LANES - 2 * nhd,), F32)]).reshape(1, LANES)
    prev = lambda i: jnp.maximum(i - 1, 0)
    row = lambda w, dt: (jax.ShapeDtypeStruct((n, w), dt), pl.BlockSpec((tm, w), lambda i: (prev(i), 0)))
    col = lambda w, dt: (jax.ShapeDtypeStruct((w, n), dt), pl.BlockSpec((w, tm), lambda i: (0, prev(i))))
    outs = [row(width, BF16), row(width, BF16), col(width, BF16), row(width, BF16), row(width, BF16),
            col(2 * nhd, F32)]
    return dict(
        name="gdn_in", norm_g=g,
        args=[w1, conv_w, lane_vec(a_log), lane_vec(dt_bias)],
        in_specs=[_const_spec(w1.shape), _const_spec(conv_w.shape), _const_spec((1, LANES)), _const_spec((1, LANES))],
        out_shape=[o[0] for o in outs], out_specs=[o[1] for o in outs],
        scratch=[pltpu.VMEM((SUBLANES, 3 * width), F32)],
        prepare=functools.partial(_gdn_in_prepare, tiles_per_seq=seq // tm),
        stages=functools.partial(_gdn_in_stages, scan_steps=int(np.log2(CHUNK)), cc=cc))


def _gdn_body(q_ref, k_ref, kt_ref, v_ref, sg_ref, onorm_ref, g_ref, b_ref, gle_ref, glo_ref,
              o_ref, mq_ref, no_ref, *, hb, pairs_per_step):
    seq = q_ref.shape[1]
    hd = GDN_HEAD_DIM
    pair = 2 * CHUNK
    n_pairs = seq // pair
    ri = lax.broadcasted_iota(jnp.int32, (CHUNK, pair), 0)
    li = lax.broadcasted_iota(jnp.int32, (CHUNK, pair), 1)
    cj = li % CHUNK
    lane_lo = li < CHUNK
    row_lo = lax.broadcasted_iota(jnp.int32, (pair, pair), 0) < CHUNK
    eye = (ri == cj).astype(F32)

    def pack(x):
        return jnp.where(lane_lo, x[:CHUNK], x[CHUNK:])

    def block_diag(x):
        return jnp.concatenate([jnp.where(lane_lo, x, 0.0), jnp.where(lane_lo, 0.0, x)], axis=0).astype(BF16)

    def pair_stages(it):
        items = [(it * pairs_per_step + j, hh) for j in range(pairs_per_step) for hh in range(hb)]
        loaded = []
        for p, hh in items:
            rows = pl.ds(pl.multiple_of(p * pair, pair), pair)
            vec = pl.ds(p, 1)
            hc = slice(hh * hd, (hh + 1) * hd)
            gl_row = jnp.where(lane_lo[:1], gle_ref[0, hh, vec, :], glo_ref[0, hh, vec, :])
            loaded.append((k_ref[0, rows, hc], q_ref[0, rows, hc], v_ref[0, rows, hc], kt_ref[hc, rows],
                           g_ref[0, hh, vec, :], b_ref[0, hh, vec, :], gl_row))
        heads = range(len(loaded))
        k2, q2, v2, kt2, g_row, b_row, gl_row = zip(*loaded)
        g_col = [jnp.broadcast_to(g_row[h], (pair, pair)).T for h in heads]
        b_col = [jnp.broadcast_to(b_row[h], (pair, pair)).T for h in heads]
        decay = [jnp.where(ri >= cj, jnp.exp(pack(g_col[h]) - g_row[h]), 0.0) for h in heads]
        gram = [_dot(jnp.concatenate([k2[h][:CHUNK], q2[h][:CHUNK], k2[h][CHUNK:], q2[h][CHUNK:]], axis=0),
                     kt2[h]) for h in heads]
        kk = [jnp.where(lane_lo, gram[h][:CHUNK], gram[h][2 * CHUNK:3 * CHUNK]) for h in heads]
        qk = [jnp.where(lane_lo, gram[h][CHUNK:2 * CHUNK], gram[h][3 * CHUNK:]) for h in heads]
        pw = [jnp.where(ri > cj, -(kk[h] * pack(b_col[h]) * decay[h]), 0.0) for h in heads]
        intra = [qk[h] * decay[h] for h in heads]
        yield
        t_mat = [eye + pw[h] for h in heads]
        n_sq = int(np.log2(CHUNK))
        for it in range(n_sq):
            rhs = [block_diag(pw[h]) for h in heads]
            if it == 0:
                pw = [_dot(pw[h].astype(BF16), rhs[h]) for h in heads]
            elif it < n_sq - 1:
                res = [_dot(jnp.concatenate([pw[h], t_mat[h]], axis=0).astype(BF16), rhs[h]) for h in heads]
                pw = [res[h][:CHUNK] for h in heads]
                t_mat = [t_mat[h] + res[h][CHUNK:] for h in heads]
            else:
                t_mat = [t_mat[h] + _dot(t_mat[h].astype(BF16), rhs[h]) for h in heads]
            yield
        ub =[_dot(block_diag(t_mat[h] * b_row[h]), v2[h]).astype(BF16) for h in heads]
        wb = [_dot(block_diag(t_mat[h] * (b_row[h] * jnp.exp(g_row[h]))), k2[h]).astype(BF16) for h in heads]
        yield
        kt_t = [(kt2[h].astype(F32) * jnp.exp(gl_row[h] - g_row[h])).astype(BF16) for h in heads]
        zero = jnp.zeros_like(wb[0])
        rhs4 = [jnp.concatenate([jnp.where(row_lo, wb[h], zero), jnp.where(row_lo, ub[h], zero),
                                 jnp.where(row_lo, zero, wb[h]), jnp.where(row_lo, zero, ub[h])], axis=1)
                for h in heads]
        mn = [_dot(kt_t[h], rhs4[h]) for h in heads]
        iwu = [_dot(block_diag(intra[h]), jnp.concatenate([wb[h], ub[h]], axis=1)) for h in heads]
        q_eff = [q2[h].astype(F32) * jnp.exp(g_col[h]) - iwu[h][:, :hd] for h in heads]
        yield
        return items, list(zip(mn, iwu, q_eff))

    def store_pairs(items, results):
        for (p, hh), (mn, iwu, q_eff) in zip(items, results):
            for half in range(2):
                ch = 2 * p + half
                rs = slice(half * CHUNK, (half + 1) * CHUNK)
                mq_ref[hh, ch, :hd, :] = mn[:, 2 * half * hd:(2 * half + 1) * hd].astype(BF16)
                mq_ref[hh, ch, hd:, :] = q_eff[rs].astype(BF16)
                no_ref[hh, ch, :hd, :] = mn[:, (2 * half + 1) * hd:(2 * half + 2) * hd]
                no_ref[hh, ch, hd:, :] = iwu[rs, hd:]

    def scan_stages(it, states):
        for j in range(pairs_per_step):
            p = it * pairs_per_step + j
            for half, gl_ref in enumerate((gle_ref, glo_ref)):
                ch = 2 * p + half
                rows = pl.ds(pl.multiple_of(ch * CHUNK, CHUNK), CHUNK)
                res = [_dot(mq_ref[hh, ch], states[hh].astype(BF16)) for hh in range(hb)]
                outs = []
                for hh in range(hb):
                    outs.append(res[hh][hd:] + no_ref[hh, ch, hd:, :])
                    states[hh] = (states[hh] * jnp.exp(gl_ref[0, hh, pl.ds(p, 1), :]) - res[hh][:hd]
                                  + no_ref[hh, ch, :hd, :])
                for hh in range(hb):
                    hc = slice(hh * hd, (hh + 1) * hd)
                    o_ref[0, rows, hc] = (_rms_rows(outs[hh], onorm_ref[...])
                                          * sg_ref[0, rows, hc].astype(F32)).astype(o_ref.dtype)
                yield

    def run(pair_gen, scan_gen):
        stored = None
        while pair_gen is not None or scan_gen is not None:
            if pair_gen is not None:
                try:
                    next(pair_gen)
                except StopIteration as done:
                    stored, pair_gen = done.value, None
            if scan_gen is not None:
                try:
                    next(scan_gen)
                except StopIteration:
                    scan_gen = None
        if stored is not None:
            store_pairs(*stored)

    n_steps = n_pairs // pairs_per_step
    run(pair_stages(0), None)

    def merged_step(it, states):
        states = list(states)
        run(pair_stages(it), scan_stages(it - 1, states))
        return tuple(states)

    states = lax.fori_loop(1, n_steps, merged_step, tuple(jnp.zeros((hd, hd), F32) for _ in range(hb)))
    run(None, scan_stages(n_steps - 1, list(states)))


def _gdn(q, k, kt, v, sg, out_norm, bg):
    b, seq, _ = q.shape
    nh, hd = GDN_HEADS, GDN_HEAD_DIM
    hb = GDN_HEAD_BLOCK
    pair = 2 * CHUNK
    n_pairs = seq // pair
    bg = bg.reshape(2, nh, b, n_pairs, pair).transpose(0, 2, 1, 3, 4)
    b_rows, g_rows = bg[0], bg[1]
    g_last = g_rows.reshape(b, nh, n_pairs, 2, CHUNK)[..., CHUNK - 1]
    gl_even = jnp.broadcast_to(g_last[..., 0:1], (b, nh, n_pairs, pair))
    gl_odd = jnp.broadcast_to(g_last[..., 1:2], (b, nh, n_pairs, pair))
    tok = pl.BlockSpec((1, seq, hb * hd), lambda bi, hi: (bi, 0, hi))
    vec = pl.BlockSpec((1, hb, n_pairs, pair), lambda bi, hi: (bi, hi, 0, 0))
    return pl.pallas_call(
        functools.partial(_gdn_body, hb=hb, pairs_per_step=GDN_PAIRS_PER_STEP),
        out_shape=jax.ShapeDtypeStruct((b, seq, nh * hd), BF16),
        grid=(b, nh // hb),
        in_specs=[tok, tok, pl.BlockSpec((hb * hd, seq), lambda bi, hi: (hi, bi)), tok, tok,
                  pl.BlockSpec((1, hd), lambda bi, hi: (0, 0)), vec, vec, vec, vec],
        out_specs=tok,
        scratch_shapes=[pltpu.VMEM((hb, seq // CHUNK, hd + CHUNK, hd), BF16),
                        pltpu.VMEM((hb, seq // CHUNK, hd + CHUNK, hd), F32)],
        compiler_params=_params(("parallel", "parallel")),
        name="gdn",
    )(q, k, kt, v, sg, out_norm.reshape(1, hd), g_rows, b_rows, gl_even, gl_odd)


def kernel(x, ffn1_norm, ffn1_w13, ffn1_w2, mix_norm, attn_w_in, fox_f_bias, mla_q_norm, mla_w_uq,
           mla_kv_norm, mla_w_ukv, attn_w_out, gdn_w_in, gdn_conv_w, gdn_a_log, gdn_dt_bias,
           gdn_out_norm, gdn_w_out, ffn2_norm, ffn2_w13, ffn2_w2, final_norm):
    b, seq, d = x.shape
    n = b * seq
    depth = ffn1_norm.shape[0]
    x2 = x.reshape(n, d)
    w13_a, w2_a, w13_b, w2_b = (w.astype(BF16) for w in (ffn1_w13, ffn1_w2, ffn2_w13, ffn2_w2))
    for layer in range(depth):
        i = layer // 2
        sh = lambda t: t.reshape(b, seq, -1)
        if layer % 2 == 0:
            x2 = _ffn(x2, layer, ffn1_norm, w13_a, w2_a)
            qf, kf, vf, qm, km, vm = _attn_in(
                x2, seq, mix_norm[layer], attn_w_in[i], fox_f_bias[i], mla_q_norm[i], mla_w_uq[i],
                mla_kv_norm[i], mla_w_ukv[i])
            o_a = _attention(qf, sh(kf), vf, FOX_HEADS, LANES, FOX_HEAD_DIM, False)
            o_b = _attention(qm, sh(km), vm, MLA_HEADS, LANES, MLA_V_DIM, True)
            wf = FOX_HEADS * FOX_HEAD_DIM
            w_out = attn_w_out[i].astype(BF16)
            mix, mix_w = [o_a.reshape(n, -1), o_b.reshape(n, -1)], [w_out[:wf], w_out[wf:]]
        else:
            x2, (q, k, kt, v, sg, bg) = _ffn(
                x2, layer, ffn1_norm, w13_a, w2_a,
                follow=_gdn_in_follow(n, d, seq, mix_norm[layer], gdn_w_in[i], gdn_conv_w[i],
                                      gdn_a_log[i], gdn_dt_bias[i]))
            o = _gdn(sh(q), sh(k), kt, sh(v), sh(sg), gdn_out_norm[i], bg)
            mix, mix_w = [o.reshape(n, -1)], [gdn_w_out[i].astype(BF16)]
        x2 = _ffn(x2, layer, ffn2_norm, w13_b, w2_b, mix, mix_w, final_norm if layer == depth - 1 else None)
    return x2.reshape(b, seq, d)
```

```python
import functools

import numpy as np
import jax
import jax.numpy as jnp
from jax import lax
from jax.experimental import pallas as pl
from jax.experimental.pallas import tpu as pltpu

F32 = jnp.float32
BF16 = jnp.bfloat16

CHUNK = 64
RMS_EPS = 1e-6
FOX_HEADS = 8
FOX_HEAD_DIM = 64
MLA_HEADS = 8
MLA_Q_RANK = 256
MLA_KV_RANK = 128
MLA_NOPE_DIM = 64
MLA_ROPE_DIM = 32
MLA_V_DIM = 64
ROPE_THETA = 10000.0
GDN_HEADS = 8
GDN_HEAD_DIM = 128
GDN_CONV = 4

LANES = 128
SUBLANES = 8
VMEM_LIMIT = 56 * 1024 * 1024
ROW_TILE = 512
FFN_ROW_TILE = 1024
ATTN_Q_TILE = 256
ATTN_HEAD_BLOCK = 4
ATTN_PIPELINE_DEPTH = 12
GDN_HEAD_BLOCK = 4
GDN_PAIRS_PER_STEP = 4
GDN_COL_CHUNK = 512
NEG = float(jnp.finfo(jnp.float32).min)
LOG2E = float(np.log2(np.e))
ONES_ROWS = 16
N_FORGET_PIECES = 3


def _params(sem):
    return pltpu.CompilerParams(dimension_semantics=sem, vmem_limit_bytes=VMEM_LIMIT)


def _rms_rows(x, g):
    return x * lax.rsqrt(jnp.mean(x * x, axis=-1, keepdims=True) + RMS_EPS) * g


def _dot(a, b):
    return jnp.dot(a, b, preferred_element_type=F32)


def _dot_nt(a, b):
    return lax.dot_general(a, b, (((1,), (1,)), ((), ())), preferred_element_type=F32)


def _const_spec(shape):
    return pl.BlockSpec(shape, lambda *_: (0,) * len(shape), pipeline_mode=pl.Buffered(1))


def _row_scan(y, n_steps, period=None):
    rows = lax.broadcasted_iota(jnp.int32, y.shape, 0)
    if period is not None:
        rows = rows % period
    shift = 1
    for _ in range(n_steps):
        y = y + jnp.where(rows >= shift, pltpu.roll(y, shift, axis=0), 0.0)
        shift *= 2
    return y


def _alternate(*gens):
    gens = list(gens)
    while gens:
        for gen in list(gens):
            try:
                next(gen)
            except StopIteration:
                gens.remove(gen)


def _ffn_body(*refs, d_ff, n_chunks, n_mix, final, follow):
    x_ref, refs = refs[0], refs[1:]
    mix_refs, wo_refs, refs = refs[:n_mix], refs[n_mix:2 * n_mix], refs[2 * n_mix:]
    g_ref, w13_ref, w2_ref = refs[:3]
    refs = refs[3:]
    fg_ref = None
    if final:
        fg_ref, refs = refs[0], refs[1:]
    if follow is not None:
        fg2_ref, refs = refs[0], refs[1:]
        f_in, refs = refs[:follow["n_in"]], refs[follow["n_in"]:]
        o_ref, refs = refs[0], refs[1:]
        f_out, refs = refs[:follow["n_out"]], refs[follow["n_out"]:]
        t_ref, hprev_ref, f_scratch = refs[0], refs[1], refs[2:]
        step = pl.program_id(0)

        @pl.when(step == 0)
        def _():
            hprev_ref[...] = jnp.zeros_like(hprev_ref)

        follow["prepare"](step - 1, f_scratch)
    else:
        o_ref, t_ref = refs

    def ffn_stages():
        x = x_ref[...]
        for m_ref, wo_ref in zip(mix_refs, wo_refs):
            x = x + _dot(m_ref[...], wo_ref[...])
        h = _rms_rows(x, g_ref[...]).astype(BF16)
        fc = d_ff // n_chunks
        for j in range(n_chunks):
            a = _dot(h, w13_ref[:, j * fc:(j + 1) * fc])
            b = _dot(h, w13_ref[:, d_ff + j * fc:d_ff + (j + 1) * fc])
            t_ref[:, j * fc:(j + 1) * fc] = (a * jax.nn.sigmoid(a) * b).astype(BF16)
            yield
        out = x + 0.5 * _dot(t_ref[...], w2_ref[...])
        if follow is not None:
            hprev_ref[...] = _rms_rows(out, fg2_ref[...]).astype(BF16)
        if final:
            out = _rms_rows(out, fg_ref[...])
        o_ref[...] = out

    if follow is None:
        _alternate(ffn_stages())
    else:
        _alternate(ffn_stages(), follow["stages"](hprev_ref[...], f_in, f_out, f_scratch))


def _ffn(x2, layer, g, w13, w2, mix=(), w_out=(), final_g=None, follow=None):
    n, d = x2.shape
    d_ff = w2.shape[1]
    tm = min(ROW_TILE if follow is not None else FFN_ROW_TILE, n)
    n_tiles = n // tm
    final = final_g is not None
    cur = (lambda i: jnp.minimum(i, n_tiles - 1)) if follow is not None else (lambda i: i)
    row = lambda w: pl.BlockSpec((tm, w), lambda i: (cur(i), 0))
    layer_spec = lambda *shape: pl.BlockSpec((None,) + shape, lambda i: (layer,) + (0,) * len(shape),
                                             pipeline_mode=pl.Buffered(1))
    args = [x2, *mix, *w_out, g.reshape(-1, 1, d), w13, w2]
    specs = ([row(d)] + [row(m.shape[1]) for m in mix] + [_const_spec(w.shape) for w in w_out]
             + [layer_spec(1, d), layer_spec(d, 2 * d_ff), layer_spec(d_ff, d)])
    if final:
        args.append(final_g.reshape(1, d))
        specs.append(_const_spec((1, d)))
    out_shape, out_specs = [jax.ShapeDtypeStruct((n, d), F32)], [row(d)]
    scratch = [pltpu.VMEM((tm, d_ff), BF16)]
    body_follow = None
    if follow is not None:
        args += [follow["norm_g"].reshape(1, d), *follow["args"]]
        specs += [_const_spec((1, d)), *follow["in_specs"]]
        out_shape += follow["out_shape"]
        out_specs += follow["out_specs"]
        scratch += [pltpu.VMEM((tm, d), BF16), *follow["scratch"]]
        body_follow = dict(n_in=len(follow["args"]), n_out=len(follow["out_shape"]),
                           prepare=follow["prepare"], stages=follow["stages"])
    outs = pl.pallas_call(
        functools.partial(_ffn_body, d_ff=d_ff, n_chunks=d_ff // 256, n_mix=len(mix), final=final,
                          follow=body_follow),
        out_shape=out_shape,
        grid=(n_tiles + (follow is not None),),
        in_specs=specs,
        out_specs=out_specs,
        scratch_shapes=scratch,
        compiler_params=_params(("parallel",) if follow is None else ("arbitrary",)),
        name="ffn" if follow is None else "ffn_" + follow["name"],
    )(*args)
    return outs[0] if follow is None else (outs[0], outs[1:])


def _attn_in_body(x_ref, g_ref, w_ref, fb_ref, qn_ref, wq_ref, kvn_ref, wkv_ref,
                  ck_ref, sk_ref, cq_ref, sq_ref, place_ref,
                  qf_ref, kf_ref, vf_ref, qm_ref, km_ref, vm_ref, carry_ref,
                  *, tiles_per_seq, scan_steps):
    wf = FOX_HEADS * FOX_HEAD_DIM
    hdim = FOX_HEAD_DIM
    tm = x_ref.shape[0]
    h = _rms_rows(x_ref[...], g_ref[...]).astype(BF16)
    y = _dot(h, w_ref[...])
    q_t = (y[:, :wf] * (hdim ** -0.5 * LOG2E)).T
    spare = jnp.where(lax.broadcasted_iota(jnp.int32, (LANES - hdim, tm), 0) < N_FORGET_PIECES, -1.0, 0.0)
    qf_ref[...] = jnp.concatenate(
        [blk for hh in range(FOX_HEADS) for blk in (q_t[hh * hdim:(hh + 1) * hdim], spare)], axis=0).astype(BF16)
    ones_rows = jnp.where(lax.broadcasted_iota(jnp.int32, (ONES_ROWS, tm), 0) == 0, 1.0, 0.0)
    with_ones = lambda vt, dv: jnp.concatenate(
        [blk for hh in range(vt.shape[0] // dv) for blk in (vt[hh * dv:(hh + 1) * dv], ones_rows)], axis=0)
    vf_ref[...] = with_ones(y[:, 2 * wf:3 * wf].T, hdim).astype(BF16)
    o = 3 * wf
    c_q = y[:, o:o + MLA_Q_RANK]
    o += MLA_Q_RANK
    c_kv = y[:, o:o + MLA_KV_RANK]
    o += MLA_KV_RANK
    kp = y[:, o:o + LANES]
    f_logit = kp

    z = f_logit + fb_ref[...]
    log_f = jnp.minimum(z, 0.0) - jnp.log1p(jnp.exp(-jnp.abs(z)))

    @pl.when(pl.program_id(0) % tiles_per_seq == 0)
    def _():
        carry_ref[...] = jnp.zeros_like(carry_ref)

    f_cum = _row_scan(log_f, scan_steps) + carry_ref[...]
    carry_ref[...] = f_cum[-1:, :]
    lane = lax.broadcasted_iota(jnp.int32, (tm, LANES), 1)
    f_scaled = jnp.where(lane < FOX_HEADS, f_cum * LOG2E, 0.0)
    f_hi = f_scaled.astype(BF16).astype(F32)
    r1 = f_scaled - f_hi
    f_mid = r1.astype(BF16).astype(F32)
    f_lo = (r1 - f_mid).astype(BF16).astype(F32)
    pieces = f_hi + pltpu.roll(f_mid, FOX_HEADS, axis=1) + pltpu.roll(f_lo, 2 * FOX_HEADS, axis=1)
    placed = _dot(pieces.astype(BF16), place_ref[...])
    low = lane < hdim
    for j in range(wf // LANES):
        kk = y[:, wf + j * LANES:wf + (j + 1) * LANES]
        for hh, src in enumerate((kk, pltpu.roll(kk, hdim, axis=1))):
            cols = slice((2 * j + hh) * LANES, (2 * j + hh + 1) * LANES)
            kf_ref[:, cols] = jnp.where(low, src, placed[:, cols]).astype(BF16)

    nh = MLA_HEADS
    swap = lambda t: pltpu.roll(t, t.shape[1] - MLA_ROPE_DIM, axis=1)
    qb = _dot(_rms_rows(c_q, qn_ref[...]).astype(BF16), wq_ref[...])
    cq = jnp.concatenate([cq_ref[...]] * nh, axis=1)
    sq = jnp.concatenate([sq_ref[...]] * nh, axis=1)
    qm_ref[...] = (qb * cq + swap(qb) * sq).T.astype(BF16)
    kvb = _dot(_rms_rows(c_kv, kvn_ref[...]).astype(BF16), wkv_ref[...])
    k_rot = kp * ck_ref[...] + swap(kp) * sk_ref[...]
    km_ref[...] = (kvb[:, :nh * LANES] + jnp.concatenate([k_rot] * nh, axis=1)).astype(BF16)
    vm_ref[...] = with_ones(kvb[:, nh * LANES:].T, MLA_V_DIM).astype(BF16)


def _attn_in(x2, seq, g, w_in, f_bias, q_norm, w_uq, kv_norm, w_ukv):
    n, d = x2.shape
    tm = min(ROW_TILE, seq)
    wf = FOX_HEADS * FOX_HEAD_DIM
    half = MLA_ROPE_DIM // 2
    nh = MLA_HEADS

    qa, ka, va, fl, cq, ckv, kpe = jnp.split(
        w_in, np.cumsum([wf, wf, wf, FOX_HEADS, MLA_Q_RANK, MLA_KV_RANK]).tolist(), axis=1)
    z = lambda c: jnp.zeros((d, c), F32)
    p_blk = jnp.concatenate([fl, z(MLA_NOPE_DIM - FOX_HEADS), kpe, kpe[:, half:], kpe[:, :half]], axis=1)
    w1 = jnp.concatenate([qa, ka, va, cq, ckv, p_blk], axis=1).astype(BF16)
    n1 = w1.shape[1]
    fb = jnp.concatenate([f_bias, jnp.zeros((LANES - FOX_HEADS,), F32)]).reshape(1, LANES)
    n_piece = N_FORGET_PIECES
    src = (np.arange(n_piece)[:, None] * FOX_HEADS + np.arange(FOX_HEADS)[None, :]).ravel()
    dst = (np.arange(FOX_HEADS)[None, :] * LANES + FOX_HEAD_DIM + np.arange(n_piece)[:, None]).ravel()
    place_np = np.zeros((LANES, FOX_HEADS * LANES), np.float32)
    place_np[src, dst] = 1.0
    place = jnp.asarray(place_np, BF16)

    dq = MLA_NOPE_DIM + MLA_ROPE_DIM
    wq3 = w_uq.reshape(MLA_Q_RANK, nh, dq)
    wq = jnp.concatenate([wq3, wq3[:, :, MLA_NOPE_DIM + half:], wq3[:, :, MLA_NOPE_DIM:MLA_NOPE_DIM + half]],
                         axis=2).reshape(MLA_Q_RANK, nh * LANES).astype(BF16)
    wkv3 = w_ukv.reshape(MLA_KV_RANK, nh, MLA_NOPE_DIM + MLA_V_DIM)
    wk = jnp.concatenate([wkv3[:, :, :MLA_NOPE_DIM],
                          jnp.zeros((MLA_KV_RANK, nh, LANES - MLA_NOPE_DIM), F32)], axis=2)
    wkv = jnp.concatenate([wk.reshape(MLA_KV_RANK, nh * LANES),
                           wkv3[:, :, MLA_NOPE_DIM:].reshape(MLA_KV_RANK, nh * MLA_V_DIM)],
                          axis=1).astype(BF16)

    inv = ROPE_THETA ** (-jnp.arange(half, dtype=F32) / half)
    ang = jnp.arange(seq).astype(F32)[:, None] * inv[None, :]
    cos, sin = jnp.cos(ang), jnp.sin(ang)
    zl = lambda c: jnp.zeros((seq, c), F32)
    pad = LANES - dq
    c_tab = jnp.concatenate([zl(MLA_NOPE_DIM), cos, cos, zl(pad)], axis=1)
    s_tab = jnp.concatenate([zl(MLA_NOPE_DIM), -sin, sin, zl(pad)], axis=1)
    scale = dq ** -0.5 * LOG2E
    cq_tab = jnp.concatenate([jnp.full((seq, MLA_NOPE_DIM), scale, F32), scale * cos, scale * cos, zl(pad)], axis=1)
    sq_tab = scale * s_tab

    tps = seq // tm
    row = lambda w: pl.BlockSpec((tm, w), lambda i: (i, 0))
    tab = pl.BlockSpec((tm, LANES), lambda i: (i % tps, 0))
    col = lambda w: pl.BlockSpec((w, tm), lambda i: (0, i))
    outs = pl.pallas_call(
        functools.partial(_attn_in_body, tiles_per_seq=tps, scan_steps=int(np.log2(tm))),
        out_shape=[jax.ShapeDtypeStruct((FOX_HEADS * LANES, n), BF16), jax.ShapeDtypeStruct((n, FOX_HEADS * LANES), BF16),
                   jax.ShapeDtypeStruct((FOX_HEADS * (FOX_HEAD_DIM + ONES_ROWS), n), BF16),
                   jax.ShapeDtypeStruct((nh * LANES, n), BF16), jax.ShapeDtypeStruct((n, nh * LANES), BF16),
                   jax.ShapeDtypeStruct((nh * (MLA_V_DIM + ONES_ROWS), n), BF16)],
        grid=(n // tm,),
        in_specs=[row(d), _const_spec((1, d)), _const_spec((d, n1)), _const_spec((1, LANES)),
                  _const_spec((1, MLA_Q_RANK)), _const_spec(wq.shape),
                  _const_spec((1, MLA_KV_RANK)), _const_spec(wkv.shape), tab, tab, tab, tab,
                  _const_spec(place.shape)],
        out_specs=[col(FOX_HEADS * LANES), row(FOX_HEADS * LANES), col(FOX_HEADS * (FOX_HEAD_DIM + ONES_ROWS)),
                   col(nh * LANES), row(nh * LANES), col(nh * (MLA_V_DIM + ONES_ROWS))],
        scratch_shapes=[pltpu.VMEM((1, LANES), F32)],
        compiler_params=_params(("arbitrary",)),
        name="attn_in",
    )(x2, g.reshape(1, d), w1, fb, q_norm.reshape(1, -1), wq, kv_norm.reshape(1, -1), wkv,
      c_tab, s_tab, cq_tab, sq_tab, place)
    return outs


def _attn_body(qt_ref, k_ref, vt_ref, o_ref, *, hb, dq, dv, tq, chunk_mask):
    seq = k_ref.shape[1]
    dvp = dv + ONES_ROWS
    r = lax.broadcasted_iota(jnp.int32, (tq, tq), 0)
    c = lax.broadcasted_iota(jnp.int32, (tq, tq), 1)
    mask = (r // CHUNK <= c // CHUNK) if chunk_mask else (r <= c)

    def scores(i, hh, j):
        kl = slice(hh * dq, (hh + 1) * dq)
        s = _dot(k_ref[0, j * tq:(j + 1) * tq, kl], qt_ref[kl, i * tq:(i + 1) * tq])
        if j == i:
            s = jnp.where(mask, s, NEG)
        m = jnp.max(s, axis=0, keepdims=True)
        return jnp.exp2(s - m).astype(BF16), m

    def values(hh, j, p):
        return _dot(vt_ref[hh * dvp:(hh + 1) * dvp, j * tq:(j + 1) * tq], p)

    def combine(parts):
        m_all = functools.reduce(jnp.maximum, [m for m, _ in parts])
        acc = sum(o * jnp.exp2(m - m_all) for m, o in parts)
        return acc[:dv] / acc[dv:dv + 1]

    units = [(i, hh, j) for i in range(seq // tq) for hh in range(hb) for j in range(i + 1)]
    pending, parts, outs = [], [], []
    for step in range(len(units) + ATTN_PIPELINE_DEPTH):
        if step < len(units):
            pending.append((units[step], scores(*units[step])))
        if step >= ATTN_PIPELINE_DEPTH:
            (i, hh, j), (p, m) = pending.pop(0)
            parts.append((m, values(hh, j, p)))
            if j == i:
                outs.append(combine(parts))
                parts = []
                if hh == hb - 1:
                    o_ref[0, i * tq:(i + 1) * tq, :] = jnp.concatenate(outs, axis=0).T.astype(o_ref.dtype)
                    outs = []


def _attention(qt, k, vt, n_heads, dq, dv, chunk_mask):
    b, seq, _ = k.shape
    hb = ATTN_HEAD_BLOCK
    tq = min(ATTN_Q_TILE, seq)
    tok = lambda w: pl.BlockSpec((1, seq, hb * w), lambda bi, hi: (bi, 0, hi))
    feat = lambda w: pl.BlockSpec((hb * w, seq), lambda bi, hi: (hi, bi))
    return pl.pallas_call(
        functools.partial(_attn_body, hb=hb, dq=dq, dv=dv, tq=tq, chunk_mask=chunk_mask),
        out_shape=jax.ShapeDtypeStruct((b, seq, n_heads * dv), BF16),
        grid=(b, n_heads // hb),
        in_specs=[feat(dq), tok(dq), feat(dv + ONES_ROWS)],
        out_specs=tok(dv),
        compiler_params=_params(("parallel", "parallel")),
        name="mla_attn" if chunk_mask else "fox_attn",
    )(qt, k, vt)


def _gdn_in_prepare(tile, scratch, *, tiles_per_seq):
    carry_ref, = scratch

    @pl.when(tile % tiles_per_seq == 0)
    def _():
        carry_ref[...] = jnp.zeros_like(carry_ref)


def _gdn_in_stages(h, ins, outs, scratch, *, scan_steps, cc):
    w_ref, cw_ref, alog_ref, dt_ref = ins
    q_ref, k_ref, kt_ref, v_ref, sg_ref, bg_ref = outs
    carry_ref, = scratch
    hd = GDN_HEAD_DIM
    width = GDN_HEADS * hd
    tm = h.shape[0]
    nv = tm // SUBLANES
    sub = lax.broadcasted_iota(jnp.int32, (nv, SUBLANES, cc), 1)
    for part, out_ref in enumerate((q_ref, k_ref, v_ref)):
        for c in range(width // cc):
            cols = slice(part * width + c * cc, part * width + (c + 1) * cc)
            y = _dot(h, w_ref[:, cols])
            prev = carry_ref[:, cols]
            carry_ref[:, cols] = y[tm - SUBLANES:]
            acc = y * cw_ref[GDN_CONV - 1:GDN_CONV, cols]
            y3 = y.reshape(nv, SUBLANES, cc)
            for j in range(1, GDN_CONV):
                rot = pltpu.roll(y3, j, axis=1)
                rot_prev = jnp.concatenate([pltpu.roll(prev, j, axis=0)[None], rot[:-1]], axis=0)
                shifted = jnp.where(sub < j, rot_prev, rot).reshape(tm, cc)
                acc = acc + shifted * cw_ref[GDN_CONV - 1 - j:GDN_CONV - j, cols]
            a = acc * jax.nn.sigmoid(acc)
            if part < 2:
                heads = []
                for hh in range(cc // hd):
                    xh = a[:, hh * hd:(hh + 1) * hd]
                    xh = xh * lax.rsqrt(jnp.sum(xh * xh, axis=-1, keepdims=True) + 1e-6)
                    heads.append(xh * hd ** -0.5 if part == 0 else xh)
                a = jnp.concatenate(heads, axis=1)
            out_ref[:, c * cc:(c + 1) * cc] = a.astype(BF16)
            if part == 1:
                kt_ref[c * cc:(c + 1) * cc, :] = a.T.astype(BF16)
            yield
    for c in range(width // cc):
        gate = _dot(h, w_ref[:, 3 * width + c * cc:3 * width + (c + 1) * cc])
        sg_ref[:, c * cc:(c + 1) * cc] = (gate * jax.nn.sigmoid(gate)).astype(BF16)
        yield
    small = _dot(h, w_ref[:, 4 * width:])
    beta = jax.nn.sigmoid(small)
    z = small + dt_ref[...]
    softplus = jnp.maximum(z, 0.0) + jnp.log1p(jnp.exp(-jnp.abs(z)))
    g_log = -jnp.exp(alog_ref[...]) * softplus
    g_cum = _row_scan(g_log, scan_steps, period=CHUNK)
    lanes = lax.broadcasted_iota(jnp.int32, small.shape, 1)
    bg_ref[...] = jnp.where(lanes < GDN_HEADS, beta, g_cum).T[:2 * GDN_HEADS]


def _gdn_in_follow(n, d, seq, g, w_in, conv_w, a_log, dt_bias):
    tm = min(ROW_TILE, seq)
    nhd = GDN_HEADS
    width = nhd * GDN_HEAD_DIM
    cc = min(GDN_COL_CHUNK, width)
    small = jnp.concatenate([w_in[:, 3 * width:3 * width + 2 * nhd], jnp.zeros((d, LANES - 2 * nhd), F32)], axis=1)
    w1 = jnp.concatenate([w_in[:, :3 * width], w_in[:, 3 * width + 2 * nhd:], small], axis=1).astype(BF16)
    lane_vec = lambda v: jnp.concatenate(
        [jnp.zeros((nhd,), F32), v, jnp.zeros((LANES - 2 * nhd,), F32)]).reshape(1, LANES)
    prev = lambda i: jnp.maximum(i - 1, 0)
    row = lambda w, dt: (jax.ShapeDtypeStruct((n, w), dt), pl.BlockSpec((tm, w), lambda i: (prev(i), 0)))
    col = lambda w, dt: (jax.ShapeDtypeStruct((w, n), dt), pl.BlockSpec((w, tm), lambda i: (0, prev(i))))
    outs = [row(width, BF16), row(width, BF16), col(width, BF16), row(width, BF16), row(width, BF16),
            col(2 * nhd, F32)]
    return dict(
        name="gdn_in", norm_g=g,
        args=[w1, conv_w, lane_vec(a_log), lane_vec(dt_bias)],
        in_specs=[_const_spec(w1.shape), _const_spec(conv_w.shape), _const_spec((1, LANES)), _const_spec((1, LANES))],
        out_shape=[o[0] for o in outs], out_specs=[o[1] for o in outs],
        scratch=[pltpu.VMEM((SUBLANES, 3 * width), F32)],
        prepare=functools.partial(_gdn_in_prepare, tiles_per_seq=seq // tm),
        stages=functools.partial(_gdn_in_stages, scan_steps=int(np.log2(CHUNK)), cc=cc))


def _gdn_body(q_ref, k_ref, kt_ref, v_ref, sg_ref, onorm_ref, g_ref, b_ref, gle_ref, glo_ref,
              o_ref, mq_ref, no_ref, *, hb, pairs_per_step):
    seq = q_ref.shape[1]
    hd = GDN_HEAD_DIM
    pair = 2 * CHUNK
    n_pairs = seq // pair
    ri = lax.broadcasted_iota(jnp.int32, (CHUNK, pair), 0)
    li = lax.broadcasted_iota(jnp.int32, (CHUNK, pair), 1)
    cj = li % CHUNK
    lane_lo = li < CHUNK
    row_lo = lax.broadcasted_iota(jnp.int32, (pair, pair), 0) < CHUNK
    eye = (ri == cj).astype(F32)

    def pack(x):
        return jnp.where(lane_lo, x[:CHUNK], x[CHUNK:])

    def block_diag(x):
        return jnp.concatenate([jnp.where(lane_lo, x, 0.0), jnp.where(lane_lo, 0.0, x)], axis=0).astype(BF16)

    def pair_stages(it):
        items = [(it * pairs_per_step + j, hh) for j in range(pairs_per_step) for hh in range(hb)]
        loaded = []
        for p, hh in items:
            rows = pl.ds(pl.multiple_of(p * pair, pair), pair)
            vec = pl.ds(p, 1)
            hc = slice(hh * hd, (hh + 1) * hd)
            gl_row = jnp.where(lane_lo[:1], gle_ref[0, hh, vec, :], glo_ref[0, hh, vec, :])
            loaded.append((k_ref[0, rows, hc], q_ref[0, rows, hc], v_ref[0, rows, hc], kt_ref[hc, rows],
                           g_ref[0, hh, vec, :], b_ref[0, hh, vec, :], gl_row))
        heads = range(len(loaded))
        k2, q2, v2, kt2, g_row, b_row, gl_row = zip(*loaded)
        g_col = [jnp.broadcast_to(g_row[h], (pair, pair)).T for h in heads]
        b_col = [jnp.broadcast_to(b_row[h], (pair, pair)).T for h in heads]
        decay = [jnp.where(ri >= cj, jnp.exp(pack(g_col[h]) - g_row[h]), 0.0) for h in heads]
        gram = [_dot(jnp.concatenate([k2[h][:CHUNK], q2[h][:CHUNK], k2[h][CHUNK:], q2[h][CHUNK:]], axis=0),
                     kt2[h]) for h in heads]
        kk = [jnp.where(lane_lo, gram[h][:CHUNK], gram[h][2 * CHUNK:3 * CHUNK]) for h in heads]
        qk = [jnp.where(lane_lo, gram[h][CHUNK:2 * CHUNK], gram[h][3 * CHUNK:]) for h in heads]
        pw = [jnp.where(ri > cj, -(kk[h] * pack(b_col[h]) * decay[h]), 0.0) for h in heads]
        intra = [qk[h] * decay[h] for h in heads]
        yield
        t_mat = [eye + pw[h] for h in heads]
        n_sq = int(np.log2(CHUNK))
        for it in range(n_sq):
            rhs = [block_diag(pw[h]) for h in heads]
            if it == 0:
                pw = [_dot(pw[h].astype(BF16), rhs[h]) for h in heads]
            elif it < n_sq - 1:
                res = [_dot(jnp.concatenate([pw[h], t_mat[h]], axis=0).astype(BF16), rhs[h]) for h in heads]
                pw = [res[h][:CHUNK] for h in heads]
                t_mat = [t_mat[h] + res[h][CHUNK:] for h in heads]
            else:
                t_mat = [t_mat[h] + _dot(t_mat[h].astype(BF16), rhs[h]) for h in heads]
            yield
        ub =[_dot(block_diag(t_mat[h] * b_row[h]), v2[h]).astype(BF16) for h in heads]
        wb = [_dot(block_diag(t_mat[h] * (b_row[h] * jnp.exp(g_row[h]))), k2[h]).astype(BF16) for h in heads]
        yield
        kt_t = [(kt2[h].astype(F32) * jnp.exp(gl_row[h] - g_row[h])).astype(BF16) for h in heads]
        zero = jnp.zeros_like(wb[0])
        rhs4 = [jnp.concatenate([jnp.where(row_lo, wb[h], zero), jnp.where(row_lo, ub[h], zero),
                                 jnp.where(row_lo, zero, wb[h]), jnp.where(row_lo, zero, ub[h])], axis=1)
                for h in heads]
        mn = [_dot(kt_t[h], rhs4[h]) for h in heads]
        iwu = [_dot(block_diag(intra[h]), jnp.concatenate([wb[h], ub[h]], axis=1)) for h in heads]
        q_eff = [q2[h].astype(F32) * jnp.exp(g_col[h]) - iwu[h][:, :hd] for h in heads]
        yield
        return items, list(zip(mn, iwu, q_eff))

    def store_pairs(items, results):
        for (p, hh), (mn, iwu, q_eff) in zip(items, results):
            for half in range(2):
                ch = 2 * p + half
                rs = slice(half * CHUNK, (half + 1) * CHUNK)
                mq_ref[hh, ch, :hd, :] = mn[:, 2 * half * hd:(2 * half + 1) * hd].astype(BF16)
                mq_ref[hh, ch, hd:, :] = q_eff[rs].astype(BF16)
                no_ref[hh, ch, :hd, :] = mn[:, (2 * half + 1) * hd:(2 * half + 2) * hd]
                no_ref[hh, ch, hd:, :] = iwu[rs, hd:]

    def scan_stages(it, states):
        for j in range(pairs_per_step):
            p = it * pairs_per_step + j
            for half, gl_ref in enumerate((gle_ref, glo_ref)):
                ch = 2 * p + half
                rows = pl.ds(pl.multiple_of(ch * CHUNK, CHUNK), CHUNK)
                res = [_dot(mq_ref[hh, ch], states[hh].astype(BF16)) for hh in range(hb)]
                outs = []
                for hh in range(hb):
                    outs.append(res[hh][hd:] + no_ref[hh, ch, hd:, :])
                    states[hh] = (states[hh] * jnp.exp(gl_ref[0, hh, pl.ds(p, 1), :]) - res[hh][:hd]
                                  + no_ref[hh, ch, :hd, :])
                for hh in range(hb):
                    hc = slice(hh * hd, (hh + 1) * hd)
                    o_ref[0, rows, hc] = (_rms_rows(outs[hh], onorm_ref[...])
                                          * sg_ref[0, rows, hc].astype(F32)).astype(o_ref.dtype)
                yield

    def run(pair_gen, scan_gen):
        stored = None
        while pair_gen is not None or scan_gen is not None:
            if pair_gen is not None:
                try:
                    next(pair_gen)
                except StopIteration as done:
                    stored, pair_gen = done.value, None
            if scan_gen is not None:
                try:
                    next(scan_gen)
                except StopIteration:
                    scan_gen = None
        if stored is not None:
            store_pairs(*stored)

    n_steps = n_pairs // pairs_per_step
    run(pair_stages(0), None)

    def merged_step(it, states):
        states = list(states)
        run(pair_stages(it), scan_stages(it - 1, states))
        return tuple(states)

    states = lax.fori_loop(1, n_steps, merged_step, tuple(jnp.zeros((hd, hd), F32) for _ in range(hb)))
    run(None, scan_stages(n_steps - 1, list(states)))


def _gdn(q, k, kt, v, sg, out_norm, bg):
    b, seq, _ = q.shape
    nh, hd = GDN_HEADS, GDN_HEAD_DIM
    hb = GDN_HEAD_BLOCK
    pair = 2 * CHUNK
    n_pairs = seq // pair
    bg = bg.reshape(2, nh, b, n_pairs, pair).transpose(0, 2, 1, 3, 4)
    b_rows, g_rows = bg[0], bg[1]
    g_last = g_rows.reshape(b, nh, n_pairs, 2, CHUNK)[..., CHUNK - 1]
    gl_even = jnp.broadcast_to(g_last[..., 0:1], (b, nh, n_pairs, pair))
    gl_odd = jnp.broadcast_to(g_last[..., 1:2], (b, nh, n_pairs, pair))
    tok = pl.BlockSpec((1, seq, hb * hd), lambda bi, hi: (bi, 0, hi))
    vec = pl.BlockSpec((1, hb, n_pairs, pair), lambda bi, hi: (bi, hi, 0, 0))
    return pl.pallas_call(
        functools.partial(_gdn_body, hb=hb, pairs_per_step=GDN_PAIRS_PER_STEP),
        out_shape=jax.ShapeDtypeStruct((b, seq, nh * hd), BF16),
        grid=(b, nh // hb),
        in_specs=[tok, tok, pl.BlockSpec((hb * hd, seq), lambda bi, hi: (hi, bi)), tok, tok,
                  pl.BlockSpec((1, hd), lambda bi, hi: (0, 0)), vec, vec, vec, vec],
        out_specs=tok,
        scratch_shapes=[pltpu.VMEM((hb, seq // CHUNK, hd + CHUNK, hd), BF16),
                        pltpu.VMEM((hb, seq // CHUNK, hd + CHUNK, hd), F32)],
        compiler_params=_params(("parallel", "parallel")),
        name="gdn",
    )(q, k, kt, v, sg, out_norm.reshape(1, hd), g_rows, b_rows, gl_even, gl_odd)


def kernel(x, ffn1_norm, ffn1_w13, ffn1_w2, mix_norm, attn_w_in, fox_f_bias, mla_q_norm, mla_w_uq,
           mla_kv_norm, mla_w_ukv, attn_w_out, gdn_w_in, gdn_conv_w, gdn_a_log, gdn_dt_bias,
           gdn_out_norm, gdn_w_out, ffn2_norm, ffn2_w13, ffn2_w2, final_norm):
    b, seq, d = x.shape
    n = b * seq
    depth = ffn1_norm.shape[0]
    x2 = x.reshape(n, d)
    w13_a, w2_a, w13_b, w2_b = (w.astype(BF16) for w in (ffn1_w13, ffn1_w2, ffn2_w13, ffn2_w2))
    for layer in range(depth):
        i = layer // 2
        sh = lambda t: t.reshape(b, seq, -1)
        if layer % 2 == 0:
            x2 = _ffn(x2, layer, ffn1_norm, w13_a, w2_a)
            qf, kf, vf, qm, km, vm = _attn_in(
                x2, seq, mix_norm[layer], attn_w_in[i], fox_f_bias[i], mla_q_norm[i], mla_w_uq[i],
                mla_kv_norm[i], mla_w_ukv[i])
            o_a = _attention(qf, sh(kf), vf, FOX_HEADS, LANES, FOX_HEAD_DIM, False)
            o_b = _attention(qm, sh(km), vm, MLA_HEADS, LANES, MLA_V_DIM, True)
            wf = FOX_HEADS * FOX_HEAD_DIM
            w_out = attn_w_out[i].astype(BF16)
            mix, mix_w = [o_a.reshape(n, -1), o_b.reshape(n, -1)], [w_out[:wf], w_out[wf:]]
        else:
            x2, (q, k, kt, v, sg, bg) = _ffn(
                x2, layer, ffn1_norm, w13_a, w2_a,
                follow=_gdn_in_follow(n, d, seq, mix_norm[layer], gdn_w_in[i], gdn_conv_w[i],
                                      gdn_a_log[i], gdn_dt_bias[i]))
            o = _gdn(sh(q), sh(k), kt, sh(v), sh(sg), gdn_out_norm[i], bg)
            mix, mix_w = [o.reshape(n, -1)], [gdn_w_out[i].astype(BF16)]
        x2 = _ffn(x2, layer, ffn2_norm, w13_b, w2_b, mix, mix_w, final_norm if layer == depth - 1 else None)
    return x2.reshape(b, seq, d)
```
